```python
import jax, jax.numpy as jnp
from jax import lax
import numpy as np

D_MODEL = 2048
BATCH = 8
SEQ = 2048
DEPTH = 4

GRID_W = 64
CTX_LEN = 256
HEAD_DIM = 128
A_HEADS = 8
A_KV_HEADS = 2
WINDOW = 128
A_BLOCK = 128
B_HEADS = 8
B_KV_HEADS = 2
Q_BLOCK = 128
C_HEADS = 8
C_DK = 128
C_DV = 128
C_CHUNK = 64
A_WIDTH = A_HEADS * HEAD_DIM
B_WIDTH = B_HEADS * HEAD_DIM
C_WIDTH = C_HEADS * C_DV
N_BRANCH = 3
ROPE_THETA = 10000.0
NORM_EPS = 1e-6
IN_SPLITS = (A_WIDTH, A_KV_HEADS * HEAD_DIM, A_KV_HEADS * HEAD_DIM, A_WIDTH,
             B_WIDTH, B_KV_HEADS * HEAD_DIM, B_KV_HEADS * HEAD_DIM, B_WIDTH,
             C_HEADS * C_DK, C_HEADS * C_DK, C_HEADS * C_DK, C_WIDTH, C_WIDTH,
             N_BRANCH * D_MODEL)
W_IN_TOTAL = sum(IN_SPLITS)

kernel_name = "hybrid_gated_branch_dit_block"

F32 = jnp.float32


def rms_norm(x, g):
    xf = x.astype(F32)
    y = xf * lax.rsqrt(jnp.mean(xf * xf, axis=-1, keepdims=True) + NORM_EPS)
    return (y * g.astype(F32)).astype(x.dtype)


def axial_rope_tables(n):
    rows = n // GRID_W
    row = jnp.repeat(jnp.arange(rows, dtype=F32), GRID_W)
    col = jnp.tile(jnp.arange(GRID_W, dtype=F32), rows)
    n_freq = HEAD_DIM // 4
    inv_freq = ROPE_THETA ** (-jnp.arange(n_freq, dtype=F32) / n_freq)
    ang = jnp.concatenate([row[:, None] * inv_freq, col[:, None] * inv_freq], axis=-1)
    ang = jnp.concatenate([ang, ang], axis=-1)
    return jnp.cos(ang), jnp.sin(ang)


def apply_rope(x, cos, sin):
    xf = x.astype(F32)
    x1, x2 = jnp.split(xf, 2, axis=-1)
    rot = jnp.concatenate([-x2, x1], axis=-1)
    return (xf * cos[:, None] + rot * sin[:, None]).astype(x.dtype)


def sink_softmax(s, sink):
    col = jnp.broadcast_to(sink[None, :, :, None, None], s.shape[:-1] + (1,))
    return jax.nn.softmax(jnp.concatenate([s, col], axis=-1), axis=-1)[..., :-1]


def window_sink_attention(q_c, k_c, v_c, q_l, k_l, v_l, sink, with_ctx):
    b, n = q_l.shape[:2]
    m = q_c.shape[1]
    g = A_HEADS // A_KV_HEADS
    scale = HEAD_DIM ** -0.5
    sink = sink.astype(F32).reshape(A_KV_HEADS, g)
    nb = n // A_BLOCK
    qb = jnp.moveaxis(q_l.reshape(b, nb, A_BLOCK, A_KV_HEADS, g, HEAD_DIM), 1, 0)

    def band(t):
        tb = jnp.pad(t, ((0, 0), (A_BLOCK, A_BLOCK), (0, 0), (0, 0)))
        tb = tb.reshape(b, nb + 2, A_BLOCK, A_KV_HEADS, HEAD_DIM)
        return jnp.moveaxis(jnp.concatenate([tb[:, :-2], tb[:, 1:-1], tb[:, 2:]], axis=2), 1, 0)

    kw, vw = band(k_l), band(v_l)
    blk = jnp.arange(nb)[:, None, None]
    qpos = blk * A_BLOCK + jnp.arange(A_BLOCK)[None, :, None]
    kpos = (blk - 1) * A_BLOCK + jnp.arange(3 * A_BLOCK)[None, None, :]
    valid = (jnp.abs(qpos - kpos) <= WINDOW) & (kpos >= 0) & (kpos < n)

    def block(args):
        qblk, kblk, vblk, mask = args
        s_win = jnp.einsum('bqhgd,bkhd->bhgqk', qblk, kblk).astype(F32) * scale
        s_win = jnp.where(mask, s_win, -jnp.inf)
        s_ctx = jnp.einsum('bqhgd,bkhd->bhgqk', qblk, k_c).astype(F32) * scale
        p = sink_softmax(jnp.concatenate([s_win, s_ctx], axis=-1), sink).astype(vblk.dtype)
        return (jnp.einsum('bhgqk,bkhd->bqhgd', p[..., :3 * A_BLOCK], vblk)
                + jnp.einsum('bhgqk,bkhd->bqhgd', p[..., 3 * A_BLOCK:], v_c))

    o_l = jnp.moveaxis(lax.map(block, (qb, kw, vw, valid)), 0, 1).reshape(b, n, A_WIDTH)
    o_c = None
    if with_ctx:
        qc = q_c.reshape(b, m, A_KV_HEADS, g, HEAD_DIM)
        s = jnp.einsum('bqhgd,bkhd->bhgqk', qc, k_c).astype(F32) * scale
        p = sink_softmax(s, sink).astype(v_c.dtype)
        o_c = jnp.einsum('bhgqk,bkhd->bqhgd', p, v_c).reshape(b, m, A_WIDTH)
    return o_c, o_l


def dense_qknorm_attention(q_c, k_c, v_c, q_l, k_l, v_l, with_ctx):
    b, n = q_l.shape[:2]
    m = q_c.shape[1]
    g = B_HEADS // B_KV_HEADS
    scale = HEAD_DIM ** -0.5
    k_all = jnp.concatenate([k_c, k_l], axis=1)
    v_all = jnp.concatenate([v_c, v_l], axis=1)
    nb = n // Q_BLOCK
    qb = jnp.moveaxis(q_l.reshape(b, nb, Q_BLOCK, B_KV_HEADS, g, HEAD_DIM), 1, 0)

    def block(qblk):
        s = jnp.einsum('bqhgd,bkhd->bhgqk', qblk, k_all).astype(F32) * scale
        p = jax.nn.softmax(s, axis=-1).astype(v_all.dtype)
        return jnp.einsum('bhgqk,bkhd->bqhgd', p, v_all)

    o_l = jnp.moveaxis(lax.map(block, qb), 0, 1).reshape(b, n, B_WIDTH)
    o_c = None
    if with_ctx:
        qc = q_c.reshape(b, m, B_KV_HEADS, g, HEAD_DIM)
        s = jnp.einsum('bqhgd,bkhd->bhgqk', qc, k_c).astype(F32) * scale
        p = jax.nn.softmax(s, axis=-1).astype(v_c.dtype)
        o_c = jnp.einsum('bhgqk,bkhd->bqhgd', p, v_c).reshape(b, m, B_WIDTH)
    return o_c, o_l


def hgrn2_gates(z, lb):
    log_f = jnp.logaddexp(jnp.log(lb), jnp.log1p(-lb) + jax.nn.log_sigmoid(z))
    k = (1.0 - lb) * jax.nn.sigmoid(-z)
    return log_f, k


def chunk_scan(q, k, v, log_f, s0):
    b, L, h, dk = q.shape
    nc = L // C_CHUNK

    def chunks(t):
        return t.reshape(b, nc, C_CHUNK, h, t.shape[-1]).transpose(1, 0, 3, 2, 4)

    tri = jnp.tril(jnp.ones((C_CHUNK, C_CHUNK), dtype=bool))[:, :, None]

    def step(S, inp):
        qc, kc, vc, gc = inp
        cum = jnp.cumsum(gc, axis=2)
        rel = cum[:, :, :, None, :] - cum[:, :, None, :, :]
        decay = jnp.exp(jnp.where(tri, rel, -jnp.inf))
        attn = jnp.einsum('bhtk,bhsk,bhtsk->bhts', qc, kc, decay)
        o = attn @ vc + jnp.einsum('bhtk,bhkv->bhtv', qc * jnp.exp(cum), S)
        tot = cum[:, :, -1:, :]
        S = (jnp.exp(tot[:, :, 0, :, None]) * S
             + jnp.einsum('bhsk,bhsv->bhkv', kc * jnp.exp(tot - cum), vc))
        return S, o

    S, o = lax.scan(step, s0, (chunks(q), chunks(k), chunks(v), chunks(log_f)))
    o = o.transpose(1, 0, 3, 2, 4).reshape(b, L, h, v.shape[-1])
    return o, S


def hgrn2_bidirectional(q, i, z_f, z_b, lb, gain, m, with_ctx):
    b, t, h, dk = q.shape
    dv = i.shape[-1]
    q = q.astype(F32) * dk ** -0.5
    i = i.astype(F32)
    s0 = jnp.zeros((b, h, dk, dv), F32)
    outs = []
    for z, lb_d, rev in ((z_f, lb[0], False), (z_b, lb[1], True)):
        log_f, k = hgrn2_gates(z.astype(F32), lb_d.reshape(h, dk))
        flip = (lambda a: jnp.flip(a, axis=1)) if rev else (lambda a: a)
        o_c, s_c = chunk_scan(flip(q[:, :m]), flip(k[:, :m]), flip(i[:, :m]), flip(log_f[:, :m]), s0)
        o_l, _ = chunk_scan(flip(q[:, m:]), flip(k[:, m:]), flip(i[:, m:]), flip(log_f[:, m:]), s_c)
        outs.append(jnp.concatenate([flip(o_c), flip(o_l)], axis=1) if with_ctx else flip(o_l))
    o = rms_norm(outs[0] + outs[1], gain)
    return o.reshape(b, o.shape[1], h * dv)


def join(o_c, o_l, with_ctx):
    return jnp.concatenate([o_c, o_l], axis=1) if with_ctx else o_l


def hybrid_layer(x_lat, x_ctx, c, c_ctx, ada_w, ada_b, norm_g, w_in, a_sink, b_qn, b_kn,
                 lb, c_gn, w_ba, w_bb, w_bc, w_out, cos, sin, with_ctx):
    b, n, _ = x_lat.shape
    m = x_ctx.shape[1]
    sh_l, sc_l, gt_l = jnp.split(jax.nn.silu(c) @ ada_w + ada_b, 3, axis=-1)
    sh_c, sc_c, gt_c = jnp.split(jax.nn.silu(c_ctx) @ ada_w + ada_b, 3, axis=-1)
    h_lat = rms_norm(x_lat, norm_g) * (1.0 + sc_l[:, None]) + sh_l[:, None]
    h_ctx = rms_norm(x_ctx, norm_g) * (1.0 + sc_c) + sh_c
    h = jnp.concatenate([h_ctx, h_lat], axis=1)
    w_parts = jnp.split(w_in, np.cumsum(IN_SPLITS)[:-1].tolist(), axis=1)
    (a_q, a_k, a_v, a_z, b_q, b_k, b_v, b_z,
     c_q, c_zf, c_zb, c_i, c_z, br_gate) = [h @ w for w in w_parts]

    def heads(t, nh, d=HEAD_DIM):
        return t.reshape(b, m + n, nh, d)

    aq, ak, av = heads(a_q, A_HEADS), heads(a_k, A_KV_HEADS), heads(a_v, A_KV_HEADS)
    oa_c, oa_l = window_sink_attention(aq[:, :m], ak[:, :m], av[:, :m],
                                       apply_rope(aq[:, m:], cos, sin), apply_rope(ak[:, m:], cos, sin),
                                       av[:, m:], a_sink, with_ctx)
    bq = rms_norm(heads(b_q, B_HEADS), b_qn)
    bk = rms_norm(heads(b_k, B_KV_HEADS), b_kn)
    bv = heads(b_v, B_KV_HEADS)
    ob_c, ob_l = dense_qknorm_attention(bq[:, :m], bk[:, :m], bv[:, :m],
                                        apply_rope(bq[:, m:], cos, sin), apply_rope(bk[:, m:], cos, sin),
                                        bv[:, m:], with_ctx)
    oc = hgrn2_bidirectional(heads(c_q, C_HEADS, C_DK), heads(c_i, C_HEADS, C_DV),
                             heads(c_zf, C_HEADS, C_DK), heads(c_zb, C_HEADS, C_DK),
                             lb, c_gn, m, with_ctx).astype(x_lat.dtype)

    q0 = 0 if with_ctx else m
    y_a = (join(oa_c, oa_l, with_ctx) * jax.nn.silu(a_z[:, q0:])) @ w_ba
    y_b = (join(ob_c, ob_l, with_ctx) * jax.nn.silu(b_z[:, q0:])) @ w_bb
    y_c = (oc * jax.nn.silu(c_z[:, q0:])) @ w_bc
    g_a, g_b, g_c = jnp.split(jax.nn.sigmoid(br_gate[:, q0:]), 3, axis=-1)
    out = (g_a * y_a + g_b * y_b + g_c * y_c) @ w_out
    new_lat = x_lat + gt_l[:, None] * out[:, -n:]
    new_ctx = x_ctx + gt_c * out[:, :m] if with_ctx else x_ctx
    return new_lat, new_ctx


def setup_inputs(seed: int = 0) -> dict:
    key = jax.random.key(seed)
    ks = jax.random.split(key, 20)
    d = D_MODEL

    def nrm(k, shape, scale):
        return jax.random.normal(k, shape, F32) * scale

    return {
        "x": nrm(ks[0], (BATCH, SEQ, d), 1.0),
        "c": nrm(ks[1], (BATCH, d), 1.0),
        "ctx": nrm(ks[2], (BATCH, CTX_LEN, d), 1.0),
        "c_ctx": nrm(ks[3], (d,), 1.0),
        "ada_w": nrm(ks[4], (DEPTH, d, 3 * d), 0.5 * d ** -0.5),
        "ada_b": nrm(ks[5], (DEPTH, 3 * d), 0.02),
        "norm_g": 1.0 + nrm(ks[6], (DEPTH, d), 0.02),
        "w_in": nrm(ks[7], (DEPTH, d, W_IN_TOTAL), d ** -0.5),
        "a_sink": nrm(ks[8], (DEPTH, A_HEADS), 0.5),
        "b_q_norm": 1.0 + nrm(ks[9], (DEPTH, HEAD_DIM), 0.02),
        "b_k_norm": 1.0 + nrm(ks[10], (DEPTH, HEAD_DIM), 0.02),
        "c_lower_bound": nrm(ks[11], (DEPTH, 2, C_HEADS * C_DK), 0.1),
        "c_out_norm": 1.0 + nrm(ks[12], (DEPTH, C_DV), 0.02),
        "w_branch_a": nrm(ks[13], (DEPTH, A_WIDTH, d), A_WIDTH ** -0.5),
        "w_branch_b": nrm(ks[14], (DEPTH, B_WIDTH, d), B_WIDTH ** -0.5),
        "w_branch_c": nrm(ks[15], (DEPTH, C_WIDTH, d), C_WIDTH ** -0.5),
        "w_out": nrm(ks[16], (DEPTH, d, d), d ** -0.5),
        "final_norm_g": 1.0 + nrm(ks[17], (d,), 0.02),
    }


def reference(x, c, ctx, c_ctx, ada_w, ada_b, norm_g, w_in, a_sink, b_q_norm, b_k_norm,
              c_lower_bound, c_out_norm, w_branch_a, w_branch_b, w_branch_c, w_out, final_norm_g):
    n = x.shape[1]
    cos, sin = axial_rope_tables(n)
    lb_all = jnp.cumsum(jax.nn.softmax(c_lower_bound.astype(F32), axis=0), axis=0)
    lb_all = lb_all - lb_all[0:1]
    h_lat, h_ctx = x, ctx
    for l in range(DEPTH):
        h_lat, h_ctx = hybrid_layer(h_lat, h_ctx, c, c_ctx, ada_w[l], ada_b[l], norm_g[l], w_in[l],
                                    a_sink[l], b_q_norm[l], b_k_norm[l], lb_all[l], c_out_norm[l],
                                    w_branch_a[l], w_branch_b[l], w_branch_c[l], w_out[l],
                                    cos, sin, l < DEPTH - 1)
    return rms_norm(h_lat, final_norm_g)
```

```python
import functools

import numpy as np
import jax
import jax.numpy as jnp
from jax import lax
from jax.experimental import pallas as pl
from jax.experimental.pallas import tpu as pltpu

F32 = jnp.float32
BF16 = jnp.bfloat16

HEAD_DIM = 128
A_HEADS, A_KV_HEADS = 8, 2
B_HEADS, B_KV_HEADS = 8, 2
C_HEADS = 8
WINDOW = 128
GRID_W = 64
ROPE_THETA = 10000.0
NORM_EPS = 1e-6
N_BRANCH = 3
GQA_GROUP = A_HEADS // A_KV_HEADS
Q_WIDTH = A_HEADS * HEAD_DIM
KV_WIDTH = A_KV_HEADS * HEAD_DIM
GROUP_WIDTH = GQA_GROUP * HEAD_DIM

A_Q, A_K, A_V, A_Z = 0, 1024, 1280, 1536
B_Q, B_K, B_V, B_Z = 2560, 3584, 3840, 4096
C_Q, C_ZF, C_ZB, C_I, C_Z = 5120, 6144, 7168, 8192, 9216
GATES = 10240

ATTN_TQ = 256
HGRN_CHUNK = 64
MOD_ROWS = 16
NEG_BIG = -1e30
VMEM_LIMIT = 56 * 1024 * 1024


def _dot(a, b):
    return jnp.dot(a, b, preferred_element_type=F32)


def _dot_nt(a, b):
    return lax.dot_general(a, b, (((1,), (1,)), ((), ())), preferred_element_type=F32)


def _dot_tn(a, b):
    return lax.dot_general(a, b, (((0,), (0,)), ((), ())), preferred_element_type=F32)


def _rms(x, g):
    return x * lax.rsqrt(jnp.mean(x * x, axis=-1, keepdims=True) + NORM_EPS) * g


def _rope(x, cos, sin_signed):
    return x * cos + pltpu.roll(x, HEAD_DIM // 2, 1) * sin_signed


def _silu(x):
    return x * jax.nn.sigmoid(x)


def _params(*sem):
    return pltpu.CompilerParams(dimension_semantics=sem, vmem_limit_bytes=VMEM_LIMIT)


def _ada_kernel(cv_ref, w_ref, b_ref, o_ref):
    s = _silu(cv_ref[...]).astype(BF16)
    o_ref[...] = _dot(s, w_ref[...].astype(BF16)) + b_ref[...]


def _ada_call(cv, ada_w, ada_b, tn=512):
    depth, d, n3 = ada_w.shape
    return pl.pallas_call(
        _ada_kernel,
        grid=(depth, n3 // tn),
        in_specs=[
            pl.BlockSpec((MOD_ROWS, d), lambda l, j: (0, 0)),
            pl.BlockSpec((None, d, tn), lambda l, j: (l, 0, j)),
            pl.BlockSpec((None, 1, tn), lambda l, j: (l, 0, j)),
        ],
        out_specs=pl.BlockSpec((None, MOD_ROWS, tn), lambda l, j: (l, 0, j)),
        out_shape=jax.ShapeDtypeStruct((depth, MOD_ROWS, n3), F32),
        compiler_params=_params("parallel", "parallel"),
        name="ada_mod",
    )(cv, ada_w, ada_b.reshape(depth, 1, n3))


def _inproj_kernel(x_ref, g_ref, sc_ref, sh_ref, w_ref, o_ref, h_ref):
    @pl.when(pl.program_id(1) == 0)
    def _():
        h = _rms(x_ref[...], g_ref[...]) * (1.0 + sc_ref[...]) + sh_ref[...]
        h_ref[...] = h.astype(BF16)

    o_ref[...] = _dot(h_ref[...], w_ref[...]).astype(BF16)


def _row_tile(rows, rows_per_mod, cap):
    t = cap
    while rows % t or rows_per_mod % t:
        t //= 2
    return t


def _inproj_call(x, g, mod, w, rows_per_mod, mod_row0, tn=1024):
    rows, d = x.shape
    n = w.shape[1]
    tm = _row_tile(rows, rows_per_mod, 1024)

    def mod_spec(which):
        return pl.BlockSpec((None, None, 1, d),
                            lambda i, j: (mod_row0 + (i * tm) // rows_per_mod, which, 0, 0))

    return pl.pallas_call(
        _inproj_kernel,
        grid=(rows // tm, n // tn),
        in_specs=[
            pl.BlockSpec((tm, d), lambda i, j: (i, 0)),
            pl.BlockSpec((1, d), lambda i, j: (0, 0)),
            mod_spec(1),
            mod_spec(0),
            pl.BlockSpec((d, tn), lambda i, j: (0, j)),
        ],
        out_specs=pl.BlockSpec((tm, tn), lambda i, j: (i, j)),
        out_shape=jax.ShapeDtypeStruct((rows, n), BF16),
        scratch_shapes=[pltpu.VMEM((tm, d), BF16)],
        compiler_params=_params("parallel", "arbitrary"),
        name="in_proj",
    )(x, g.reshape(1, d), mod, mod, w)


def _softmax_pv(parts, extra_logit):
    m = functools.reduce(jnp.maximum, [jnp.max(s, axis=-1, keepdims=True) for s, _ in parts])
    if extra_logit is not None:
        m = jnp.maximum(m, extra_logit)
    ps = [jnp.exp(s - m) for s, _ in parts]
    l = functools.reduce(jnp.add, [jnp.sum(p, axis=-1, keepdims=True) for p in ps])
    if extra_logit is not None:
        l = l + jnp.exp(extra_logit - m)
    o = functools.reduce(jnp.add, [_dot(p.astype(BF16), v) for p, (_, v) in zip(ps, parts)])
    return o * (1.0 / l)


def _head(ref, g):
    return ref[:, g * HEAD_DIM:(g + 1) * HEAD_DIM]


def _store_gated(o_ref, z_ref, g, o):
    z = _head(z_ref, g).astype(F32)
    o_ref[:, g * HEAD_DIM:(g + 1) * HEAD_DIM] = (o * _silu(z)).astype(BF16)


def _attn_a_kernel(*refs, with_ctx, n_lat, scale):
    if with_ctx:
        (sink_ref, qc_ref, zc_ref, ql_ref, zl_ref, kc_ref, vc_ref, kl_ref, vl_ref, cos_ref, sin_ref,
         oc_ref, ol_ref, kl_s) = refs
    else:
        (sink_ref, ql_ref, zl_ref, kc_ref, vc_ref, kl_ref, vl_ref, cos_ref, sin_ref, ol_ref, kl_s) = refs
    hk = pl.program_id(1)
    qi = pl.program_id(2)
    q0 = 1 if with_ctx else 0
    tq = ATTN_TQ
    win_k = tq + 2 * WINDOW

    @pl.when(qi == 0)
    def _prep():
        def body(i, carry):
            r = pl.multiple_of(i * tq, tq)
            k = kl_ref[pl.ds(r, tq), :].astype(F32)
            kl_s[pl.ds(r, tq), :] = _rope(k, cos_ref[pl.ds(r, tq), :], sin_ref[pl.ds(r, tq), :]).astype(BF16)
            return carry
        lax.fori_loop(0, n_lat // tq, body, 0)

    if with_ctx:
        @pl.when(qi == 0)
        def _ctx():
            for g in range(GQA_GROUP):
                s = _dot_nt(_head(qc_ref, g), kc_ref[...]) * scale
                o = _softmax_pv([(s, vc_ref[...])], sink_ref[hk * GQA_GROUP + g])
                _store_gated(oc_ref, zc_ref, g, o)

    @pl.when(qi >= q0)
    def _lat():
        r0 = pl.multiple_of((qi - q0) * tq, tq)
        ws = pl.multiple_of(jnp.clip(r0 - WINDOW, 0, n_lat - win_k), WINDOW)
        kw = kl_s[pl.ds(ws, win_k), :]
        vw = vl_ref[pl.ds(ws, win_k), :]
        qpos = r0 + lax.broadcasted_iota(jnp.int32, (tq, win_k), 0)
        kpos = ws + lax.broadcasted_iota(jnp.int32, (tq, win_k), 1)
        valid = jnp.abs(qpos - kpos) <= WINDOW
        cos = cos_ref[pl.ds(r0, tq), :]
        sin = sin_ref[pl.ds(r0, tq), :]
        for g in range(GQA_GROUP):
            q = _rope(_head(ql_ref, g).astype(F32), cos, sin).astype(BF16)
            s_w = jnp.where(valid, _dot_nt(q, kw) * scale, NEG_BIG)
            s_c = _dot_nt(q, kc_ref[...]) * scale
            o = _softmax_pv([(s_w, vw), (s_c, vc_ref[...])], sink_ref[hk * GQA_GROUP + g])
            _store_gated(ol_ref, zl_ref, g, o)


def _attn_b_kernel(*refs, with_ctx, n_lat, scale):
    if with_ctx:
        (qc_ref, zc_ref, ql_ref, zl_ref, kc_ref, vc_ref, kl_ref, vl_ref, cos_ref, sin_ref, qn_ref, kn_ref,
         oc_ref, ol_ref, kc_s, kl_s) = refs
    else:
        (ql_ref, zl_ref, kc_ref, vc_ref, kl_ref, vl_ref, cos_ref, sin_ref, qn_ref, kn_ref,
         ol_ref, kc_s, kl_s) = refs
    qi = pl.program_id(2)
    q0 = 1 if with_ctx else 0
    tq = ATTN_TQ

    @pl.when(qi == 0)
    def _prep():
        kn = kn_ref[...] * scale
        kc_s[...] = _rms(kc_ref[...].astype(F32), kn).astype(BF16)

        def body(i, carry):
            r = pl.multiple_of(i * tq, tq)
            k = _rms(kl_ref[pl.ds(r, tq), :].astype(F32), kn)
            kl_s[pl.ds(r, tq), :] = _rope(k, cos_ref[pl.ds(r, tq), :], sin_ref[pl.ds(r, tq), :]).astype(BF16)
            return carry
        lax.fori_loop(0, n_lat // tq, body, 0)

    if with_ctx:
        @pl.when(qi == 0)
        def _ctx():
            for g in range(GQA_GROUP):
                q = _rms(_head(qc_ref, g).astype(F32), qn_ref[...]).astype(BF16)
                o = _softmax_pv([(_dot_nt(q, kc_s[...]), vc_ref[...])], None)
                _store_gated(oc_ref, zc_ref, g, o)

    @pl.when(qi >= q0)
    def _lat():
        r0 = pl.multiple_of((qi - q0) * tq, tq)
        cos = cos_ref[pl.ds(r0, tq), :]
        sin = sin_ref[pl.ds(r0, tq), :]
        for g in range(GQA_GROUP):
            q = _rope(_rms(_head(ql_ref, g).astype(F32), qn_ref[...]), cos, sin).astype(BF16)
            o = _softmax_pv([(_dot_nt(q, kc_s[...]), vc_ref[...]), (_dot_nt(q, kl_s[...]), vl_ref[...])], None)
            _store_gated(ol_ref, zl_ref, g, o)


def _attn_call(kind, p_c, p_l, extras, cos, sin, batch, with_ctx):
    m = p_c.shape[0] // batch
    n = p_l.shape[0] // batch
    tq = ATTN_TQ
    assert m == tq and n % tq == 0 and n >= tq + 2 * WINDOW
    nq = n // tq
    q0 = 1 if with_ctx else 0
    if kind == "a":
        qcol, kcol, vcol, zcol = A_Q, A_K, A_V, A_Z
    else:
        qcol, kcol, vcol, zcol = B_Q, B_K, B_V, B_Z
    gw, hd = GROUP_WIDTH, HEAD_DIM

    def lat_row(b, hk, qi):
        return b * nq + jnp.maximum(qi - q0, 0)

    def grp_c(col):
        return pl.BlockSpec((tq, gw), lambda b, hk, qi: (b, col // gw + hk))

    def grp_l(col):
        return pl.BlockSpec((tq, gw), lambda b, hk, qi: (lat_row(b, hk, qi), col // gw + hk))

    def kv(rows, col):
        return pl.BlockSpec((rows, hd), lambda b, hk, qi: (b, col // hd + hk))

    def full(a):
        return pl.BlockSpec(a.shape, lambda b, hk, qi: (0,) * a.ndim)

    in_specs, args = [], []
    if kind == "a":
        in_specs.append(pl.BlockSpec(memory_space=pltpu.SMEM))
        args.append(extras[0])
    if with_ctx:
        in_specs += [grp_c(qcol), grp_c(zcol)]
        args += [p_c, p_c]
    in_specs += [grp_l(qcol), grp_l(zcol), kv(m, kcol), kv(m, vcol), kv(n, kcol), kv(n, vcol), full(cos), full(sin)]
    args += [p_l, p_l, p_c, p_c, p_l, p_l, cos, sin]
    scratch = [pltpu.VMEM((n, hd), BF16)]
    if kind == "b":
        in_specs += [full(extras[0]), full(extras[1])]
        args += list(extras)
        scratch = [pltpu.VMEM((m, hd), BF16)] + scratch

    out_l_spec = pl.BlockSpec((tq, gw), lambda b, hk, qi: (lat_row(b, hk, qi), hk))
    out_l_shape = jax.ShapeDtypeStruct((batch * n, Q_WIDTH), BF16)
    if with_ctx:
        out_specs = [pl.BlockSpec((tq, gw), lambda b, hk, qi: (b, hk)), out_l_spec]
        out_shape = [jax.ShapeDtypeStruct((batch * m, Q_WIDTH), BF16), out_l_shape]
    else:
        out_specs, out_shape = [out_l_spec], [out_l_shape]

    body = _attn_a_kernel if kind == "a" else _attn_b_kernel
    outs = pl.pallas_call(
        functools.partial(body, with_ctx=with_ctx, n_lat=n, scale=HEAD_DIM ** -0.5),
        grid=(batch, A_KV_HEADS, nq + q0),
        in_specs=in_specs,
        out_specs=out_specs,
        out_shape=out_shape,
        scratch_shapes=scratch,
        compiler_params=_params("parallel", "parallel", "arbitrary"),
        name="attn_" + kind,
    )(*args)
    return (outs[0], outs[1]) if with_ctx else (None, outs[0])


def _hgrn_levels(chunk):
    levels, h = [], 1
    while h < chunk:
        levels.append(h)
        h *= 2
    return levels


def _hgrn_constants(chunk):
    t = np.arange(chunk)[:, None]
    s = np.arange(chunk)[None, :]
    tri = (s <= t).astype(np.float32)
    masks = [(t == s).astype(np.float32)]
    for h in _hgrn_levels(chunk):
        blk = 2 * h
        masks.append(((t // blk == s // blk) & (t % blk >= h) & (s % blk < h)).astype(np.float32))
    masks = np.stack(masks)
    tri2 = np.stack([tri, tri.T])
    masks2 = np.stack([masks, masks.transpose(0, 2, 1)])
    return jnp.asarray(tri2, dtype=BF16), jnp.asarray(masks2, dtype=F32)


def _hgrn_chunk(q, z, v, lbp, tri, msk_ref, st_ref, cum_s, rev):
    chunk = q.shape[0]
    log_lb, log1m_lb, one_m_lb = lbp
    log_sig = jnp.minimum(z, 0.0) - jnp.log1p(jnp.exp(-jnp.abs(z)))
    b = log1m_lb + log_sig
    mx = jnp.maximum(log_lb, b)
    g = mx + jnp.log(jnp.exp(log_lb - mx) + jnp.exp(b - mx))
    kk = one_m_lb * jax.nn.sigmoid(-z)
    f = 1.0 - kk

    g_hi = g.astype(BF16)
    g_lo = (g - g_hi.astype(F32)).astype(BF16)
    cc = _dot(tri, jnp.concatenate([g_hi, g_lo], axis=1))
    cum = cc[:, :HEAD_DIM] + cc[:, HEAD_DIM:]
    cum_s[...] = cum
    tot = cum_s[pl.ds(0 if rev else chunk - 1, 1), :]

    st = st_ref[...]
    o = _dot_nt((q * jnp.exp(cum)).astype(BF16), st.astype(BF16))
    k_tail = (kk * jnp.exp(tot - cum)).astype(BF16)
    st_ref[...] = st * jnp.exp(tot) + _dot_tn(v, k_tail)

    attn = _dot_nt(q.astype(BF16), kk.astype(BF16)) * msk_ref[0]
    row = lax.broadcasted_iota(jnp.int32, q.shape, 0)
    for li, h in enumerate(_hgrn_levels(chunk)):
        blk = 2 * h
        ph = row & (blk - 1)
        t_role = (ph < h) if rev else (ph >= h)
        if h == 1:
            e = jnp.where(t_role, f, 1.0)
        elif h == 2:
            f_next = pltpu.roll(f, chunk - 1, 0)
            f_prev = pltpu.roll(f, 1, 0)
            if rev:
                e = jnp.where(ph == 0, f * f_next, jnp.where(ph == 1, f, jnp.where(ph == 2, 1.0, f_prev)))
            else:
                e = jnp.where(ph == 0, f_next, jnp.where(ph == 1, 1.0, jnp.where(ph == 2, f, f_prev * f)))
        else:
            pieces = []
            for blk_i in range(chunk // blk):
                r = blk_i * blk + (h if rev else h - 1)
                pieces.append(jnp.broadcast_to(cum_s[pl.ds(r, 1), :], (blk, HEAD_DIM)))
            ref = pieces[0] if len(pieces) == 1 else jnp.concatenate(pieces, axis=0)
            e = jnp.exp(-jnp.abs(cum - ref))
        zq = (jnp.where(t_role, q, kk) * e).astype(BF16)
        attn = attn + _dot_nt(zq, zq) * msk_ref[1 + li]
    return o + _dot(attn.astype(BF16), v)


def _hgrn_kernel(*refs, with_ctx, m_ctx, n_lat):
    (qc_ref, fc_ref, bc_ref, ic_ref, ql_ref, fl_ref, bl_ref, il_ref) = refs[:8]
    refs = refs[8:]
    if with_ctx:
        zc_ref, refs = refs[0], refs[1:]
    zl_ref, lb_ref, gain_ref, tri_ref, msk_ref = refs[:5]
    refs = refs[5:]
    if with_ctx:
        oc_ref, refs = refs[0], refs[1:]
    ol_ref = refs[0]
    st_f, st_b, cum_f, cum_b, ofc, obc, ofl, obl = refs[1:]
    chunk = HGRN_CHUNK
    q_scale = HEAD_DIM ** -0.5

    st_f[...] = jnp.zeros_like(st_f)
    st_b[...] = jnp.zeros_like(st_b)

    def lb_params(d):
        lb = lb_ref[pl.ds(d, 1), :]
        return jnp.log(lb), jnp.log1p(-lb), 1.0 - lb

    lbp_f, lbp_b = lb_params(0), lb_params(1)

    def scan(q_ref, zf_ref, zb_ref, i_ref, of_ref, ob_ref, rows):
        nc = rows // chunk

        def body(c, carry):
            rf = pl.multiple_of(c * chunk, chunk)
            rb = pl.multiple_of((nc - 1 - c) * chunk, chunk)
            of_ref[pl.ds(rf, chunk), :] = _hgrn_chunk(
                q_ref[pl.ds(rf, chunk), :].astype(F32) * q_scale, zf_ref[pl.ds(rf, chunk), :].astype(F32),
                i_ref[pl.ds(rf, chunk), :], lbp_f, tri_ref[0], msk_ref.at[0], st_f, cum_f, False)
            ob_ref[pl.ds(rb, chunk), :] = _hgrn_chunk(
                q_ref[pl.ds(rb, chunk), :].astype(F32) * q_scale, zb_ref[pl.ds(rb, chunk), :].astype(F32),
                i_ref[pl.ds(rb, chunk), :], lbp_b, tri_ref[1], msk_ref.at[1], st_b, cum_b, True)
            return carry
        lax.fori_loop(0, nc, body, 0)

    scan(qc_ref, fc_ref, bc_ref, ic_ref, ofc, obc, m_ctx)
    scan(ql_ref, fl_ref, bl_ref, il_ref, ofl, obl, n_lat)

    def finish(of_ref, ob_ref, z_ref, o_ref, rows):
        blk = 256

        def body(i, carry):
            r = pl.multiple_of(i * blk, blk)
            o = _rms(of_ref[pl.ds(r, blk), :] + ob_ref[pl.ds(r, blk), :], gain_ref[...])
            o_ref[pl.ds(r, blk), :] = (o * _silu(z_ref[pl.ds(r, blk), :].astype(F32))).astype(BF16)
            return carry
        lax.fori_loop(0, rows // blk, body, 0)

    if with_ctx:
        finish(ofc, obc, zc_ref, oc_ref, m_ctx)
    finish(ofl, obl, zl_ref, ol_ref, n_lat)


def _hgrn_call(p_c, p_l, lb, gain, batch, with_ctx):
    m = p_c.shape[0] // batch
    n = p_l.shape[0] // batch
    hd = HEAD_DIM
    assert m % 256 == 0 and n % 256 == 0
    tri, masks = _hgrn_constants(HGRN_CHUNK)

    def col(rows, c):
        return pl.BlockSpec((rows, hd), lambda b, h: (b, c // hd + h))

    def full(a):
        return pl.BlockSpec(a.shape, lambda b, h: (0,) * a.ndim)

    in_specs = [col(m, C_Q), col(m, C_ZF), col(m, C_ZB), col(m, C_I),
                col(n, C_Q), col(n, C_ZF), col(n, C_ZB), col(n, C_I)]
    args = [p_c] * 4 + [p_l] * 4
    if with_ctx:
        in_specs.append(col(m, C_Z))
        args.append(p_c)
    in_specs += [col(n, C_Z), pl.BlockSpec((2, hd), lambda b, h: (0, h)), full(gain), full(tri), full(masks)]
    args += [p_l, lb, gain, tri, masks]

    out_l_spec = pl.BlockSpec((n, hd), lambda b, h: (b, h))
    out_l_shape = jax.ShapeDtypeStruct((batch * n, Q_WIDTH), BF16)
    if with_ctx:
        out_specs = [pl.BlockSpec((m, hd), lambda b, h: (b, h)), out_l_spec]
        out_shape = [jax.ShapeDtypeStruct((batch * m, Q_WIDTH), BF16), out_l_shape]
    else:
        out_specs, out_shape = [out_l_spec], [out_l_shape]

    outs = pl.pallas_call(
        functools.partial(_hgrn_kernel, with_ctx=with_ctx, m_ctx=m, n_lat=n),
        grid=(batch, C_HEADS),
        in_specs=in_specs,
        out_specs=out_specs,
        out_shape=out_shape,
        scratch_shapes=[pltpu.VMEM((hd, hd), F32), pltpu.VMEM((hd, hd), F32),
                        pltpu.VMEM((HGRN_CHUNK, hd), F32), pltpu.VMEM((HGRN_CHUNK, hd), F32),
                        pltpu.VMEM((m, hd), F32), pltpu.VMEM((m, hd), F32),
                        pltpu.VMEM((n, hd), F32), pltpu.VMEM((n, hd), F32)],
        compiler_params=_params("parallel", "parallel"),
        name="hgrn2",
    )(*args)
    return (outs[0], outs[1]) if with_ctx else (None, outs[0])


def _merge_kernel(*refs, final_norm):
    (x_ref, ga_ref, gb_ref, gc_ref, ka_ref, kb_ref, kc_ref, gt_ref, wa_ref, wb_ref, wc_ref, wo_ref) = refs[:12]
    o_ref = refs[-1]
    mix = jax.nn.sigmoid(ka_ref[...].astype(F32)) * _dot(ga_ref[...], wa_ref[...])
    mix = mix + jax.nn.sigmoid(kb_ref[...].astype(F32)) * _dot(gb_ref[...], wb_ref[...])
    mix = mix + jax.nn.sigmoid(kc_ref[...].astype(F32)) * _dot(gc_ref[...], wc_ref[...])
    new = x_ref[...] + gt_ref[...] * _dot(mix.astype(BF16), wo_ref[...])
    if final_norm:
        new = _rms(new, refs[12][...])
    o_ref[...] = new


def _merge_call(x, ga, gb, gc, p, mod, wa, wb, wc, wo, rows_per_mod, mod_row0, final_g=None):
    rows, d = x.shape
    tm = _row_tile(rows, rows_per_mod, 256)

    def rowblk(width, colblk=0):
        return pl.BlockSpec((tm, width), lambda i: (i, colblk))

    def weight(w):
        return pl.BlockSpec(w.shape, lambda i: (0, 0), pipeline_mode=pl.Buffered(1))

    in_specs = [rowblk(d), rowblk(Q_WIDTH), rowblk(Q_WIDTH), rowblk(Q_WIDTH),
                rowblk(d, GATES // d), rowblk(d, GATES // d + 1), rowblk(d, GATES // d + 2),
                pl.BlockSpec((None, None, 1, d), lambda i: (mod_row0 + (i * tm) // rows_per_mod, 2, 0, 0)),
                weight(wa), weight(wb), weight(wc), weight(wo)]
    args = [x, ga, gb, gc, p, p, p, mod, wa, wb, wc, wo]
    if final_g is not None:
        in_specs.append(pl.BlockSpec((1, d), lambda i: (0, 0)))
        args.append(final_g.reshape(1, d))
    return pl.pallas_call(
        functools.partial(_merge_kernel, final_norm=final_g is not None),
        grid=(rows // tm,),
        in_specs=in_specs,
        out_specs=rowblk(d),
        out_shape=jax.ShapeDtypeStruct((rows, d), F32),
        compiler_params=_params("parallel"),
        name="merge_out",
    )(*args)


def _rope_tables(n):
    rows = n // GRID_W
    row = jnp.repeat(jnp.arange(rows, dtype=F32), GRID_W)
    col = jnp.tile(jnp.arange(GRID_W, dtype=F32), rows)
    n_freq = HEAD_DIM // 4
    inv_freq = ROPE_THETA ** (-jnp.arange(n_freq, dtype=F32) / n_freq)
    ang = jnp.concatenate([row[:, None] * inv_freq, col[:, None] * inv_freq], axis=-1)
    ang = jnp.concatenate([ang, ang], axis=-1)
    sign = jnp.where(jnp.arange(HEAD_DIM) < HEAD_DIM // 2, -1.0, 1.0).astype(F32)
    return jnp.cos(ang), jnp.sin(ang) * sign


def kernel(x, c, ctx, c_ctx, ada_w, ada_b, norm_g, w_in, a_sink, b_q_norm, b_k_norm, c_lower_bound, c_out_norm,
           w_branch_a, w_branch_b, w_branch_c, w_out, final_norm_g):
    batch, n, d = x.shape
    m = ctx.shape[1]
    depth = ada_w.shape[0]
    assert batch + 1 <= MOD_ROWS
    cos, sin = _rope_tables(n)
    lb_all = jnp.cumsum(jax.nn.softmax(c_lower_bound.astype(F32), axis=0), axis=0)
    lb_all = lb_all - lb_all[0:1]

    cv = jnp.zeros((MOD_ROWS, d), F32).at[:batch].set(c).at[batch].set(c_ctx)
    mod = _ada_call(cv, ada_w, ada_b).reshape(depth, MOD_ROWS, 3, 1, d)

    x_l = x.reshape(batch * n, d)
    x_c = ctx.reshape(batch * m, d)
    for l in range(depth):
        with_ctx = l < depth - 1
        last = l == depth - 1
        w_l = w_in[l].astype(BF16)
        p_c = _inproj_call(x_c, norm_g[l], mod[l], w_l, batch * m, batch)
        p_l = _inproj_call(x_l, norm_g[l], mod[l], w_l, n, 0)
        ga_c, ga_l = _attn_call("a", p_c, p_l, (a_sink[l],), cos, sin, batch, with_ctx)
        gb_c, gb_l = _attn_call("b", p_c, p_l, (b_q_norm[l].reshape(1, -1), b_k_norm[l].reshape(1, -1)),
                                cos, sin, batch, with_ctx)
        gc_c, gc_l = _hgrn_call(p_c, p_l, lb_all[l], c_out_norm[l].reshape(1, -1), batch, with_ctx)
        ws = (w_branch_a[l].astype(BF16), w_branch_b[l].astype(BF16), w_branch_c[l].astype(BF16),
              w_out[l].astype(BF16))
        if with_ctx:
            x_c = _merge_call(x_c, ga_c, gb_c, gc_c, p_c, mod[l], *ws, batch * m, batch)
        x_l = _merge_call(x_l, ga_l, gb_l, gc_l, p_l, mod[l], *ws, n, 0,
                          final_g=final_norm_g if last else None)
    return x_l.reshape(batch, n, d)
```

```python
import functools

import numpy as np
import jax
import jax.numpy as jnp
from jax import lax
from jax.experimental import pallas as pl
from jax.experimental.pallas import tpu as pltpu

F32 = jnp.float32
BF16 = jnp.bfloat16

HEAD_DIM = 128
A_HEADS, A_KV_HEADS = 8, 2
B_HEADS, B_KV_HEADS = 8, 2
C_HEADS = 8
WINDOW = 128
GRID_W = 64
ROPE_THETA = 10000.0
NORM_EPS = 1e-6
N_BRANCH = 3
GQA_GROUP = A_HEADS // A_KV_HEADS
Q_WIDTH = A_HEADS * HEAD_DIM
KV_WIDTH = A_KV_HEADS * HEAD_DIM
GROUP_WIDTH = GQA_GROUP * HEAD_DIM

A_Q, A_K, A_V, A_Z = 0, 1024, 1280, 1536
B_Q, B_K, B_V, B_Z = 2560, 3584, 3840, 4096
C_Q, C_ZF, C_ZB, C_I, C_Z = 5120, 6144, 7168, 8192, 9216
GATES = 10240

ATTN_TQ = 256
HGRN_CHUNK = 64
HGRN_UNROLL = 4
MOD_ROWS = 16
NEG_BIG = -1e30
VMEM_LIMIT = 56 * 1024 * 1024


def _dot(a, b):
    return jnp.dot(a, b, preferred_element_type=F32)


def _dot_nt(a, b):
    return lax.dot_general(a, b, (((1,), (1,)), ((), ())), preferred_element_type=F32)


def _dot_tn(a, b):
    return lax.dot_general(a, b, (((0,), (0,)), ((), ())), preferred_element_type=F32)


def _rms(x, g):
    return x * lax.rsqrt(jnp.mean(x * x, axis=-1, keepdims=True) + NORM_EPS) * g


def _rope(x, cos, sin_signed):
    return x * cos + pltpu.roll(x, HEAD_DIM // 2, 1) * sin_signed


def _silu(x):
    return x * jax.nn.sigmoid(x)


def _params(*sem):
    return pltpu.CompilerParams(dimension_semantics=sem, vmem_limit_bytes=VMEM_LIMIT)


def _ada_kernel(cv_ref, w_ref, b_ref, o_ref):
    s = _silu(cv_ref[...]).astype(BF16)
    o_ref[...] = _dot(s, w_ref[...].astype(BF16)) + b_ref[...]


def _ada_call(cv, ada_w, ada_b, tn=512):
    depth, d, n3 = ada_w.shape
    return pl.pallas_call(
        _ada_kernel,
        grid=(depth, n3 // tn),
        in_specs=[
            pl.BlockSpec((MOD_ROWS, d), lambda l, j: (0, 0)),
            pl.BlockSpec((None, d, tn), lambda l, j: (l, 0, j)),
            pl.BlockSpec((None, 1, tn), lambda l, j: (l, 0, j)),
        ],
        out_specs=pl.BlockSpec((None, MOD_ROWS, tn), lambda l, j: (l, 0, j)),
        out_shape=jax.ShapeDtypeStruct((depth, MOD_ROWS, n3), F32),
        compiler_params=_params("parallel", "parallel"),
        name="ada_mod",
    )(cv, ada_w, ada_b.reshape(depth, 1, n3))


def _inproj_kernel(x_ref, g_ref, sc_ref, sh_ref, w_ref, o_ref, h_ref):
    @pl.when(pl.program_id(1) == 0)
    def _():
        h = _rms(x_ref[...], g_ref[...]) * (1.0 + sc_ref[...]) + sh_ref[...]
        h_ref[...] = h.astype(BF16)

    o_ref[...] = _dot(h_ref[...], w_ref[...]).astype(BF16)


def _row_tile(rows, rows_per_mod, cap):
    t = cap
    while rows % t or rows_per_mod % t:
        t //= 2
    return t


def _inproj_call(x, g, mod, w, rows_per_mod, mod_row0, tn=1024):
    rows, d = x.shape
    n = w.shape[1]
    tm = _row_tile(rows, rows_per_mod, 1024)

    def mod_spec(which):
        return pl.BlockSpec((None, None, 1, d),
                            lambda i, j: (mod_row0 + (i * tm) // rows_per_mod, which, 0, 0))

    return pl.pallas_call(
        _inproj_kernel,
        grid=(rows // tm, n // tn),
        in_specs=[
            pl.BlockSpec((tm, d), lambda i, j: (i, 0)),
            pl.BlockSpec((1, d), lambda i, j: (0, 0)),
            mod_spec(1),
            mod_spec(0),
            pl.BlockSpec((d, tn), lambda i, j: (0, j)),
        ],
        out_specs=pl.BlockSpec((tm, tn), lambda i, j: (i, j)),
        out_shape=jax.ShapeDtypeStruct((rows, n), BF16),
        scratch_shapes=[pltpu.VMEM((tm, d), BF16)],
        compiler_params=_params("parallel", "arbitrary"),
        name="in_proj",
    )(x, g.reshape(1, d), mod, mod, w)


def _softmax_pv(parts, extra_logit):
    m = functools.reduce(jnp.maximum, [jnp.max(s, axis=-1, keepdims=True) for s, _ in parts])
    if extra_logit is not None:
        m = jnp.maximum(m, extra_logit)
    ps = [jnp.exp(s - m) for s, _ in parts]
    l = functools.reduce(jnp.add, [jnp.sum(p, axis=-1, keepdims=True) for p in ps])
    if extra_logit is not None:
        l = l + jnp.exp(extra_logit - m)
    o = functools.reduce(jnp.add, [_dot(p.astype(BF16), v) for p, (_, v) in zip(ps, parts)])
    return o * (1.0 / l)


def _head(ref, g):
    return ref[:, g * HEAD_DIM:(g + 1) * HEAD_DIM]


def _store_gated(o_ref, z_ref, g, o):
    z = _head(z_ref, g).astype(F32)
    o_ref[:, g * HEAD_DIM:(g + 1) * HEAD_DIM] = (o * _silu(z)).astype(BF16)


def _attn_a_kernel(*refs, with_ctx, n_lat, scale):
    if with_ctx:
        (sink_ref, qc_ref, zc_ref, ql_ref, zl_ref, kc_ref, vc_ref, kl_ref, vl_ref, cos_ref, sin_ref,
         oc_ref, ol_ref, kl_s) = refs
    else:
        (sink_ref, ql_ref, zl_ref, kc_ref, vc_ref, kl_ref, vl_ref, cos_ref, sin_ref, ol_ref, kl_s) = refs
    hk = pl.program_id(1)
    qi = pl.program_id(2)
    q0 = 1 if with_ctx else 0
    tq = ATTN_TQ
    win_k = tq + 2 * WINDOW

    @pl.when(qi == 0)
    def _prep():
        def body(i, carry):
            r = pl.multiple_of(i * tq, tq)
            k = kl_ref[pl.ds(r, tq), :].astype(F32)
            kl_s[pl.ds(r, tq), :] = _rope(k, cos_ref[pl.ds(r, tq), :], sin_ref[pl.ds(r, tq), :]).astype(BF16)
            return carry
        lax.fori_loop(0, n_lat // tq, body, 0)

    if with_ctx:
        @pl.when(qi == 0)
        def _ctx():
            for g in range(GQA_GROUP):
                s = _dot_nt(_head(qc_ref, g), kc_ref[...]) * scale
                o = _softmax_pv([(s, vc_ref[...])], sink_ref[hk * GQA_GROUP + g])
                _store_gated(oc_ref, zc_ref, g, o)

    @pl.when(qi >= q0)
    def _lat():
        r0 = pl.multiple_of((qi - q0) * tq, tq)
        ws = pl.multiple_of(jnp.clip(r0 - WINDOW, 0, n_lat - win_k), WINDOW)
        kw = kl_s[pl.ds(ws, win_k), :]
        vw = vl_ref[pl.ds(ws, win_k), :]
        qpos = r0 + lax.broadcasted_iota(jnp.int32, (tq, win_k), 0)
        kpos = ws + lax.broadcasted_iota(jnp.int32, (tq, win_k), 1)
        valid = jnp.abs(qpos - kpos) <= WINDOW
        cos = cos_ref[pl.ds(r0, tq), :]
        sin = sin_ref[pl.ds(r0, tq), :]
        for g in range(GQA_GROUP):
            q = _rope(_head(ql_ref, g).astype(F32), cos, sin).astype(BF16)
            s_w = jnp.where(valid, _dot_nt(q, kw) * scale, NEG_BIG)
            s_c = _dot_nt(q, kc_ref[...]) * scale
            o = _softmax_pv([(s_w, vw), (s_c, vc_ref[...])], sink_ref[hk * GQA_GROUP + g])
            _store_gated(ol_ref, zl_ref, g, o)


def _attn_b_kernel(*refs, with_ctx, n_lat, scale):
    if with_ctx:
        (qc_ref, zc_ref, ql_ref, zl_ref, kc_ref, vc_ref, kl_ref, vl_ref, cos_ref, sin_ref, qn_ref, kn_ref,
         oc_ref, ol_ref, kc_s, kl_s) = refs
    else:
        (ql_ref, zl_ref, kc_ref, vc_ref, kl_ref, vl_ref, cos_ref, sin_ref, qn_ref, kn_ref,
         ol_ref, kc_s, kl_s) = refs
    qi = pl.program_id(2)
    q0 = 1 if with_ctx else 0
    tq = ATTN_TQ

    @pl.when(qi == 0)
    def _prep():
        kn = kn_ref[...] * scale
        kc_s[...] = _rms(kc_ref[...].astype(F32), kn).astype(BF16)

        def body(i, carry):
            r = pl.multiple_of(i * tq, tq)
            k = _rms(kl_ref[pl.ds(r, tq), :].astype(F32), kn)
            kl_s[pl.ds(r, tq), :] = _rope(k, cos_ref[pl.ds(r, tq), :], sin_ref[pl.ds(r, tq), :]).astype(BF16)
            return carry
        lax.fori_loop(0, n_lat // tq, body, 0)

    if with_ctx:
        @pl.when(qi == 0)
        def _ctx():
            for g in range(GQA_GROUP):
                q = _rms(_head(qc_ref, g).astype(F32), qn_ref[...]).astype(BF16)
                o = _softmax_pv([(_dot_nt(q, kc_s[...]), vc_ref[...])], None)
                _store_gated(oc_ref, zc_ref, g, o)

    @pl.when(qi >= q0)
    def _lat():
        r0 = pl.multiple_of((qi - q0) * tq, tq)
        cos = cos_ref[pl.ds(r0, tq), :]
        sin = sin_ref[pl.ds(r0, tq), :]
        for g in range(GQA_GROUP):
            q = _rope(_rms(_head(ql_ref, g).astype(F32), qn_ref[...]), cos, sin).astype(BF16)
            o = _softmax_pv([(_dot_nt(q, kc_s[...]), vc_ref[...]), (_dot_nt(q, kl_s[...]), vl_ref[...])], None)
            _store_gated(ol_ref, zl_ref, g, o)


def _attn_call(kind, p_c, p_l, extras, cos, sin, batch, with_ctx):
    m = p_c.shape[0] // batch
    n = p_l.shape[0] // batch
    tq = ATTN_TQ
    assert m == tq and n % tq == 0 and n >= tq + 2 * WINDOW
    nq = n // tq
    q0 = 1 if with_ctx else 0
    if kind == "a":
        qcol, kcol, vcol, zcol = A_Q, A_K, A_V, A_Z
    else:
        qcol, kcol, vcol, zcol = B_Q, B_K, B_V, B_Z
    gw, hd = GROUP_WIDTH, HEAD_DIM

    def lat_row(b, hk, qi):
        return b * nq + jnp.maximum(qi - q0, 0)

    def grp_c(col):
        return pl.BlockSpec((tq, gw), lambda b, hk, qi: (b, col // gw + hk))

    def grp_l(col):
        return pl.BlockSpec((tq, gw), lambda b, hk, qi: (lat_row(b, hk, qi), col // gw + hk))

    def kv(rows, col):
        return pl.BlockSpec((rows, hd), lambda b, hk, qi: (b, col // hd + hk))

    def full(a):
        return pl.BlockSpec(a.shape, lambda b, hk, qi: (0,) * a.ndim)

    in_specs, args = [], []
    if kind == "a":
        in_specs.append(pl.BlockSpec(memory_space=pltpu.SMEM))
        args.append(extras[0])
    if with_ctx:
        in_specs += [grp_c(qcol), grp_c(zcol)]
        args += [p_c, p_c]
    in_specs += [grp_l(qcol), grp_l(zcol), kv(m, kcol), kv(m, vcol), kv(n, kcol), kv(n, vcol), full(cos), full(sin)]
    args += [p_l, p_l, p_c, p_c, p_l, p_l, cos, sin]
    scratch = [pltpu.VMEM((n, hd), BF16)]
    if kind == "b":
        in_specs += [full(extras[0]), full(extras[1])]
        args += list(extras)
        scratch = [pltpu.VMEM((m, hd), BF16)] + scratch

    out_l_spec = pl.BlockSpec((tq, gw), lambda b, hk, qi: (lat_row(b, hk, qi), hk))
    out_l_shape = jax.ShapeDtypeStruct((batch * n, Q_WIDTH), BF16)
    if with_ctx:
        out_specs = [pl.BlockSpec((tq, gw), lambda b, hk, qi: (b, hk)), out_l_spec]
        out_shape = [jax.ShapeDtypeStruct((batch * m, Q_WIDTH), BF16), out_l_shape]
    else:
        out_specs, out_shape = [out_l_spec], [out_l_shape]

    body = _attn_a_kernel if kind == "a" else _attn_b_kernel
    outs = pl.pallas_call(
        functools.partial(body, with_ctx=with_ctx, n_lat=n, scale=HEAD_DIM ** -0.5),
        grid=(batch, A_KV_HEADS, nq + q0),
        in_specs=in_specs,
        out_specs=out_specs,
        out_shape=out_shape,
        scratch_shapes=scratch,
        compiler_params=_params("parallel", "parallel", "arbitrary"),
        name="attn_" + kind,
    )(*args)
    return (outs[0], outs[1]) if with_ctx else (None, outs[0])


HGRN_PAIRS = ((0, 32), (1, 2), (4, 8), (16, None))
HGRN_SELECT_LEVELS = (1, 2, 4, 8)
HGRN_SIGN_LEVELS = (1, 2, 4)
LOG2E = 1.4426950408889634


def _hgrn_constants():
    c = HGRN_CHUNK
    t = np.arange(c)[:, None]
    s = np.arange(c)[None, :]
    tri = s <= t

    def pair_mask(h):
        if h is None:
            return np.zeros((c, c), bool)
        if h == 0:
            return t == s
        blk = 2 * h
        return (t // blk == s // blk) & (t % blk >= h) & (s % blk < h)

    mw = np.stack([np.stack([np.concatenate([pair_mask(a).T if rev else pair_mask(a),
                                             pair_mask(b).T if rev else pair_mask(b)], axis=1)
                             for a, b in HGRN_PAIRS]) for rev in (False, True)])
    rows = np.broadcast_to(np.arange(c)[:, None], (c, HEAD_DIM))

    def t_role(h, rev):
        ph = rows % (2 * h)
        return (ph < h) if rev else (ph >= h)

    role = np.stack([np.stack([t_role(h, rev) for h in HGRN_SELECT_LEVELS]) for rev in (False, True)])
    sgn = np.stack([np.stack([np.where(t_role(h, rev), 1.0, -1.0) for h in HGRN_SIGN_LEVELS])
                    for rev in (False, True)])
    return (jnp.asarray(np.stack([tri, tri.T]), dtype=BF16), jnp.asarray(mw, dtype=BF16),
            jnp.asarray(role, dtype=BF16), jnp.asarray(sgn, dtype=F32))


def _hgrn_gates(z, lbp, tri):
    log_lb, log1m_lb, one_m_lb = lbp
    u = jnp.exp(-jnp.abs(z))
    log_sig = jnp.minimum(z, 0.0) - jnp.log(1.0 + u)
    b = log1m_lb + log_sig
    g = jnp.maximum(log_lb, b) + jnp.log(1.0 + jnp.exp(-jnp.abs(log_lb - b)))
    kk = one_m_lb * jnp.where(z >= 0.0, u, 1.0) * (1.0 / (1.0 + u))
    g_hi = g.astype(BF16)
    g_lo = (g - g_hi.astype(F32)).astype(BF16)
    cc = _dot(tri, jnp.concatenate([g_hi, g_lo], axis=1))
    return kk.astype(BF16), (cc[:, :HEAD_DIM] + cc[:, HEAD_DIM:]) * LOG2E


def _hgrn_operands(q, kk, cum_ref, r, role_ref, sgn_ref, rev):
    c = HGRN_CHUNK
    cum = cum_ref[pl.ds(r, c), :]

    def row(k):
        return cum_ref[pl.ds(r + k, 1), :]

    tot = row(0 if rev else c - 1)
    q_in = q * jnp.exp2(cum).astype(BF16)
    k_out = kk * jnp.exp2(tot - cum).astype(BF16)

    sub = lax.broadcasted_iota(jnp.int32, (8, HEAD_DIM), 0)

    def operand(h):
        blk = 2 * h
        if h >= 8:
            d_parts, o_parts = [], []
            for b in range(c // blk):
                lo, mid, hi = b * blk, b * blk + h, (b + 1) * blk
                ref = row(mid if rev else mid - 1)
                if rev:
                    d_parts += [cum[lo:mid] - ref, ref - cum[mid:hi]]
                    o_parts += [q[lo:mid], kk[mid:hi]]
                else:
                    d_parts += [ref - cum[lo:mid], cum[mid:hi] - ref]
                    o_parts += [kk[lo:mid], q[mid:hi]]
            d = jnp.concatenate(d_parts, axis=0)
            if h >= 16:
                return jnp.concatenate(o_parts, axis=0) * jnp.exp2(d).astype(BF16)
        else:
            parts = []
            for grp in range(c // 8):
                base = grp * 8
                if h == 4:
                    ref = row(base + (4 if rev else 3))
                elif h == 2:
                    ref = jnp.where(sub < 4, row(base + (2 if rev else 1)), row(base + (6 if rev else 5)))
                else:
                    k0 = base + (1 if rev else 0)
                    ref = jnp.where(sub < 2, row(k0), jnp.where(sub < 4, row(k0 + 2),
                                                               jnp.where(sub < 6, row(k0 + 4), row(k0 + 6))))
                parts.append(cum[base:base + 8] - ref)
            d = jnp.concatenate(parts, axis=0) * sgn_ref[HGRN_SIGN_LEVELS.index(h)]
        opnd = jnp.where(role_ref[HGRN_SELECT_LEVELS.index(h)] > 0, q, kk)
        return opnd * jnp.exp2(d).astype(BF16)

    def sides(item):
        if item == 0:
            return q, kk
        z = operand(item)
        return z, z

    zeros = jnp.zeros((c, HEAD_DIM), BF16)
    pairs = []
    for a, b in HGRN_PAIRS:
        la, ra = sides(a)
        if b is None:
            pairs.append((la, jnp.concatenate([ra, zeros], axis=0)))
        else:
            lb, rb = sides(b)
            pairs.append((jnp.concatenate([la, lb], axis=1),
                          jnp.concatenate([jnp.concatenate([ra, zeros], axis=1),
                                           jnp.concatenate([zeros, rb], axis=1)], axis=0)))
    return q_in, k_out, tot, pairs


def _hgrn_step(fwd, bwd, mw_ref):
    c = HGRN_CHUNK
    (qf, kf, tf, pf), vf, sf = fwd
    (qb, kb, tb, pb), vb, sb = bwd

    def blocks(x):
        return x[:c, :HEAD_DIM], x[c:, HEAD_DIM:]

    st_f, st_b = sf[...], sb[...]
    o_f, o_b = blocks(_dot_nt(jnp.concatenate([qf, qb], axis=0),
                              jnp.concatenate([st_f.astype(BF16), st_b.astype(BF16)], axis=0)))
    sf[...] = st_f * jnp.exp2(tf) + _dot_tn(vf, kf)
    sb[...] = st_b * jnp.exp2(tb) + _dot_tn(vb, kb)

    acc_f = acc_b = None
    for j in range(len(HGRN_PAIRS)):
        t_f, t_b = blocks(_dot_nt(jnp.concatenate([pf[j][0], pb[j][0]], axis=0),
                                  jnp.concatenate([pf[j][1], pb[j][1]], axis=0)))
        t_f = t_f.astype(BF16) * mw_ref[0, j]
        t_b = t_b.astype(BF16) * mw_ref[1, j]
        acc_f = t_f if acc_f is None else acc_f + t_f
        acc_b = t_b if acc_b is None else acc_b + t_b
    i_f, i_b = blocks(_dot(jnp.concatenate([acc_f, acc_b], axis=0),
                           jnp.concatenate([jnp.concatenate([vf, vf], axis=0),
                                            jnp.concatenate([vb, vb], axis=0)], axis=1)))
    return o_f + i_f, o_b + i_b


def _hgrn_kernel(*refs, with_ctx, m_ctx, n_lat):
    (qc_ref, fc_ref, bc_ref, ic_ref, ql_ref, fl_ref, bl_ref, il_ref) = refs[:8]
    refs = refs[8:]
    if with_ctx:
        zc_ref, refs = refs[0], refs[1:]
    zl_ref, lb_ref, gain_ref, tri_ref, mw_ref, role_ref, sgn_ref = refs[:7]
    refs = refs[7:]
    if with_ctx:
        oc_ref, refs = refs[0], refs[1:]
    ol_ref = refs[0]
    st_f, st_b, q_s, v_s, kkf_s, kkb_s, cumf_s, cumb_s, of_s, ob_s = refs[1:]
    chunk = HGRN_CHUNK
    nc_c, nc_l = m_ctx // chunk, n_lat // chunk

    def lb_params(d):
        lb = lb_ref[pl.ds(d, 1), :]
        return jnp.log(lb), jnp.log(1.0 - lb), 1.0 - lb

    lbp_f, lbp_b = lb_params(0), lb_params(1)

    def gates(q_ref, zf_ref, zb_ref, i_ref, n_chunks, base):
        def body(c, carry):
            r = pl.multiple_of(c * chunk, chunk)
            dst = pl.ds(pl.multiple_of(base + r, chunk), chunk)
            q_s[dst, :] = q_ref[pl.ds(r, chunk), :]
            v_s[dst, :] = i_ref[pl.ds(r, chunk), :]
            kkf_s[dst, :], cumf_s[dst, :] = _hgrn_gates(zf_ref[pl.ds(r, chunk), :].astype(F32), lbp_f, tri_ref[0])
            kkb_s[dst, :], cumb_s[dst, :] = _hgrn_gates(zb_ref[pl.ds(r, chunk), :].astype(F32), lbp_b, tri_ref[1])
            return carry
        lax.fori_loop(0, n_chunks, body, 0, unroll=2)

    gates(qc_ref, fc_ref, bc_ref, ic_ref, nc_c, 0)
    gates(ql_ref, fl_ref, bl_ref, il_ref, nc_l, m_ctx)

    st_f[...] = jnp.zeros_like(st_f)
    st_b[...] = jnp.zeros_like(st_b)
    n_chunks = nc_c + nc_l

    def body(c, carry):
        rf = pl.multiple_of(c * chunk, chunk)
        cb = jnp.where(c < nc_c, nc_c - 1 - c, n_chunks - 1 - (c - nc_c))
        rb = pl.multiple_of(cb * chunk, chunk)
        ops_f = _hgrn_operands(q_s[pl.ds(rf, chunk), :], kkf_s[pl.ds(rf, chunk), :], cumf_s, rf,
                               role_ref.at[0], sgn_ref.at[0], False)
        ops_b = _hgrn_operands(q_s[pl.ds(rb, chunk), :], kkb_s[pl.ds(rb, chunk), :], cumb_s, rb,
                               role_ref.at[1], sgn_ref.at[1], True)
        of_s[pl.ds(rf, chunk), :], ob_s[pl.ds(rb, chunk), :] = _hgrn_step(
            (ops_f, v_s[pl.ds(rf, chunk), :], st_f), (ops_b, v_s[pl.ds(rb, chunk), :], st_b), mw_ref)
        return carry
    lax.fori_loop(0, n_chunks, body, 0, unroll=HGRN_UNROLL if n_chunks % HGRN_UNROLL == 0 else 1)

    def finish(z_ref, o_ref, rows, base):
        blk = 256

        def body(i, carry):
            r = pl.multiple_of(i * blk, blk)
            src = pl.ds(pl.multiple_of(base + r, blk), blk)
            o = _rms((of_s[src, :] + ob_s[src, :]) * (HEAD_DIM ** -0.5), gain_ref[...])
            o_ref[pl.ds(r, blk), :] = (o * _silu(z_ref[pl.ds(r, blk), :].astype(F32))).astype(BF16)
            return carry
        lax.fori_loop(0, rows // blk, body, 0)

    if with_ctx:
        finish(zc_ref, oc_ref, m_ctx, 0)
    finish(zl_ref, ol_ref, n_lat, m_ctx)


def _hgrn_call(p_c, p_l, lb, gain, batch, with_ctx):
    m = p_c.shape[0] // batch
    n = p_l.shape[0] // batch
    hd = HEAD_DIM
    assert m % 256 == 0 and n % 256 == 0
    consts = _hgrn_constants()

    def col(rows, c):
        return pl.BlockSpec((rows, hd), lambda b, h: (b, c // hd + h))

    def full(a):
        return pl.BlockSpec(a.shape, lambda b, h: (0,) * a.ndim)

    in_specs = [col(m, C_Q), col(m, C_ZF), col(m, C_ZB), col(m, C_I),
                col(n, C_Q), col(n, C_ZF), col(n, C_ZB), col(n, C_I)]
    args = [p_c] * 4 + [p_l] * 4
    if with_ctx:
        in_specs.append(col(m, C_Z))
        args.append(p_c)
    in_specs += [col(n, C_Z), pl.BlockSpec((2, hd), lambda b, h: (0, h)), full(gain)] + [full(a) for a in consts]
    args += [p_l, lb, gain, *consts]

    out_l_spec = pl.BlockSpec((n, hd), lambda b, h: (b, h))
    out_l_shape = jax.ShapeDtypeStruct((batch * n, Q_WIDTH), BF16)
    if with_ctx:
        out_specs = [pl.BlockSpec((m, hd), lambda b, h: (b, h)), out_l_spec]
        out_shape = [jax.ShapeDtypeStruct((batch * m, Q_WIDTH), BF16), out_l_shape]
    else:
        out_specs, out_shape = [out_l_spec], [out_l_shape]

    outs = pl.pallas_call(
        functools.partial(_hgrn_kernel, with_ctx=with_ctx, m_ctx=m, n_lat=n),
        grid=(batch, C_HEADS),
        in_specs=in_specs,
        out_specs=out_specs,
        out_shape=out_shape,
        scratch_shapes=([pltpu.VMEM((hd, hd), F32)] * 2 + [pltpu.VMEM((m + n, hd), BF16)] * 4
                        + [pltpu.VMEM((m + n, hd), F32)] * 4),
        compiler_params=_params("parallel", "parallel"),
        name="hgrn2",
    )(*args)
    return (outs[0], outs[1]) if with_ctx else (None, outs[0])


def _merge_kernel(*refs, final_norm):
    (x_ref, ga_ref, gb_ref, gc_ref, ka_ref, kb_ref, kc_ref, gt_ref, wa_ref, wb_ref, wc_ref, wo_ref) = refs[:12]
    o_ref = refs[-1]
    mix = jax.nn.sigmoid(ka_ref[...].astype(F32)) * _dot(ga_ref[...], wa_ref[...])
    mix = mix + jax.nn.sigmoid(kb_ref[...].astype(F32)) * _dot(gb_ref[...], wb_ref[...])
    mix = mix + jax.nn.sigmoid(kc_ref[...].astype(F32)) * _dot(gc_ref[...], wc_ref[...])
    new = x_ref[...] + gt_ref[...] * _dot(mix.astype(BF16), wo_ref[...])
    if final_norm:
        new = _rms(new, refs[12][...])
    o_ref[...] = new


def _merge_call(x, ga, gb, gc, p, mod, wa, wb, wc, wo, rows_per_mod, mod_row0, final_g=None):
    rows, d = x.shape
    tm = _row_tile(rows, rows_per_mod, 256)

    def rowblk(width, colblk=0):
        return pl.BlockSpec((tm, width), lambda i: (i, colblk))

    def weight(w):
        return pl.BlockSpec(w.shape, lambda i: (0, 0), pipeline_mode=pl.Buffered(1))

    in_specs = [rowblk(d), rowblk(Q_WIDTH), rowblk(Q_WIDTH), rowblk(Q_WIDTH),
                rowblk(d, GATES // d), rowblk(d, GATES // d + 1), rowblk(d, GATES // d + 2),
                pl.BlockSpec((None, None, 1, d), lambda i: (mod_row0 + (i * tm) // rows_per_mod, 2, 0, 0)),
                weight(wa), weight(wb), weight(wc), weight(wo)]
    args = [x, ga, gb, gc, p, p, p, mod, wa, wb, wc, wo]
    if final_g is not None:
        in_specs.append(pl.BlockSpec((1, d), lambda i: (0, 0)))
        args.append(final_g.reshape(1, d))
    return pl.pallas_call(
        functools.partial(_merge_kernel, final_norm=final_g is not None),
        grid=(rows // tm,),
        in_specs=in_specs,
        out_specs=rowblk(d),
        out_shape=jax.ShapeDtypeStruct((rows, d), F32),
        compiler_params=_params("parallel"),
        name="merge_out",
    )(*args)


def _rope_tables(n):
    rows = n // GRID_W
    row = jnp.repeat(jnp.arange(rows, dtype=F32), GRID_W)
    col = jnp.tile(jnp.arange(GRID_W, dtype=F32), rows)
    n_freq = HEAD_DIM // 4
    inv_freq = ROPE_THETA ** (-jnp.arange(n_freq, dtype=F32) / n_freq)
    ang = jnp.concatenate([row[:, None] * inv_freq, col[:, None] * inv_freq], axis=-1)
    ang = jnp.concatenate([ang, ang], axis=-1)
    sign = jnp.where(jnp.arange(HEAD_DIM) < HEAD_DIM // 2, -1.0, 1.0).astype(F32)
    return jnp.cos(ang), jnp.sin(ang) * sign


def kernel(x, c, ctx, c_ctx, ada_w, ada_b, norm_g, w_in, a_sink, b_q_norm, b_k_norm, c_lower_bound, c_out_norm,
           w_branch_a, w_branch_b, w_branch_c, w_out, final_norm_g):
    batch, n, d = x.shape
    m = ctx.shape[1]
    depth = ada_w.shape[0]
    assert batch + 1 <= MOD_ROWS
    cos, sin = _rope_tables(n)
    lb_all = jnp.cumsum(jax.nn.softmax(c_lower_bound.astype(F32), axis=0), axis=0)
    lb_all = lb_all - lb_all[0:1]

    cv = jnp.zeros((MOD_ROWS, d), F32).at[:batch].set(c).at[batch].set(c_ctx)
    mod = _ada_call(cv, ada_w, ada_b).reshape(depth, MOD_ROWS, 3, 1, d)

    x_l = x.reshape(batch * n, d)
    x_c = ctx.reshape(batch * m, d)
    for l in range(depth):
        with_ctx = l < depth - 1
        last = l == depth - 1
        w_l = w_in[l].astype(BF16)
        p_c = _inproj_call(x_c, norm_g[l], mod[l], w_l, batch * m, batch)
        p_l = _inproj_call(x_l, norm_g[l], mod[l], w_l, n, 0)
        ga_c, ga_l = _attn_call("a", p_c, p_l, (a_sink[l],), cos, sin, batch, with_ctx)
        gb_c, gb_l = _attn_call("b", p_c, p_l, (b_q_norm[l].reshape(1, -1), b_k_norm[l].reshape(1, -1)),
                                cos, sin, batch, with_ctx)
        gc_c, gc_l = _hgrn_call(p_c, p_l, lb_all[l], c_out_norm[l].reshape(1, -1), batch, with_ctx)
        ws = (w_branch_a[l].astype(BF16), w_branch_b[l].astype(BF16), w_branch_c[l].astype(BF16),
              w_out[l].astype(BF16))
        if with_ctx:
            x_c = _merge_call(x_c, ga_c, gb_c, gc_c, p_c, mod[l], *ws, batch * m, batch)
        x_l = _merge_call(x_l, ga_l, gb_l, gc_l, p_l, mod[l], *ws, n, 0,
                          final_g=final_norm_g if last else None)
    return x_l.reshape(batch, n, d)
```

```python
import functools

import numpy as np
import jax
import jax.numpy as jnp
from jax import lax
from jax.experimental import pallas as pl
from jax.experimental.pallas import tpu as pltpu

F32 = jnp.float32
BF16 = jnp.bfloat16

HEAD_DIM = 128
A_HEADS, A_KV_HEADS = 8, 2
B_HEADS, B_KV_HEADS = 8, 2
C_HEADS = 8
WINDOW = 128
GRID_W = 64
ROPE_THETA = 10000.0
NORM_EPS = 1e-6
N_BRANCH = 3
GQA_GROUP = A_HEADS // A_KV_HEADS
Q_WIDTH = A_HEADS * HEAD_DIM
KV_WIDTH = A_KV_HEADS * HEAD_DIM
GROUP_WIDTH = GQA_GROUP * HEAD_DIM

A_Q, A_K, A_V, A_Z = 0, 1024, 1280, 1536
B_Q, B_K, B_V, B_Z = 2560, 3584, 3840, 4096
C_Q, C_ZF, C_ZB, C_I, C_Z = 5120, 6144, 7168, 8192, 9216
GATES = 10240

ATTN_TQ = 256
HGRN_CHUNK = 64
HGRN_UNROLL = 4
MOD_ROWS = 16
NEG_BIG = -1e30
VMEM_LIMIT = 56 * 1024 * 1024


def _dot(a, b):
    return jnp.dot(a, b, preferred_element_type=F32)


def _dot_nt(a, b):
    return lax.dot_general(a, b, (((1,), (1,)), ((), ())), preferred_element_type=F32)


def _dot_tn(a, b):
    return lax.dot_general(a, b, (((0,), (0,)), ((), ())), preferred_element_type=F32)


def _rms(x, g):
    return x * lax.rsqrt(jnp.mean(x * x, axis=-1, keepdims=True) + NORM_EPS) * g


def _rope(x, cos, sin_signed):
    return x * cos + pltpu.roll(x, HEAD_DIM // 2, 1) * sin_signed


def _silu(x):
    return x * jax.nn.sigmoid(x)


def _params(*sem, flags=None):
    return pltpu.CompilerParams(dimension_semantics=sem, vmem_limit_bytes=VMEM_LIMIT, flags=flags)


def _ada_kernel(cv_ref, w_ref, b_ref, o_ref):
    s = _silu(cv_ref[...]).astype(BF16)
    o_ref[...] = _dot(s, w_ref[...].astype(BF16)) + b_ref[...]


def _ada_call(cv, ada_w, ada_b, tn=512):
    depth, d, n3 = ada_w.shape
    return pl.pallas_call(
        _ada_kernel,
        grid=(depth, n3 // tn),
        in_specs=[
            pl.BlockSpec((MOD_ROWS, d), lambda l, j: (0, 0)),
            pl.BlockSpec((None, d, tn), lambda l, j: (l, 0, j)),
            pl.BlockSpec((None, 1, tn), lambda l, j: (l, 0, j)),
        ],
        out_specs=pl.BlockSpec((None, MOD_ROWS, tn), lambda l, j: (l, 0, j)),
        out_shape=jax.ShapeDtypeStruct((depth, MOD_ROWS, n3), F32),
        compiler_params=_params("parallel", "parallel"),
        name="ada_mod",
    )(cv, ada_w, ada_b.reshape(depth, 1, n3))


def _inproj_kernel(x_ref, g_ref, sc_ref, sh_ref, w_ref, o_ref, h_ref):
    @pl.when(pl.program_id(1) == 0)
    def _():
        h = _rms(x_ref[...], g_ref[...]) * (1.0 + sc_ref[...]) + sh_ref[...]
        h_ref[...] = h.astype(BF16)

    o_ref[...] = _dot(h_ref[...], w_ref[...]).astype(BF16)


def _row_tile(rows, rows_per_mod, cap):
    t = cap
    while rows % t or rows_per_mod % t:
        t //= 2
    return t


def _inproj_call(x, g, mod, w, rows_per_mod, mod_row0, tn=1024):
    rows, d = x.shape
    n = w.shape[1]
    tm = _row_tile(rows, rows_per_mod, 1024)

    def mod_spec(which):
        return pl.BlockSpec((None, None, 1, d),
                            lambda i, j: (mod_row0 + (i * tm) // rows_per_mod, which, 0, 0))

    return pl.pallas_call(
        _inproj_kernel,
        grid=(rows // tm, n // tn),
        in_specs=[
            pl.BlockSpec((tm, d), lambda i, j: (i, 0)),
            pl.BlockSpec((1, d), lambda i, j: (0, 0)),
            mod_spec(1),
            mod_spec(0),
            pl.BlockSpec((d, tn), lambda i, j: (0, j)),
        ],
        out_specs=pl.BlockSpec((tm, tn), lambda i, j: (i, j)),
        out_shape=jax.ShapeDtypeStruct((rows, n), BF16),
        scratch_shapes=[pltpu.VMEM((tm, d), BF16)],
        compiler_params=_params("parallel", "arbitrary"),
        name="in_proj",
    )(x, g.reshape(1, d), mod, mod, w)


def _softmax_pv(parts, extra_logit):
    m = functools.reduce(jnp.maximum, [jnp.max(s, axis=-1, keepdims=True) for s, _ in parts])
    if extra_logit is not None:
        m = jnp.maximum(m, extra_logit)
    ps = [jnp.exp(s - m) for s, _ in parts]
    l = functools.reduce(jnp.add, [jnp.sum(p, axis=-1, keepdims=True) for p in ps])
    if extra_logit is not None:
        l = l + jnp.exp(extra_logit - m)
    o = functools.reduce(jnp.add, [_dot(p.astype(BF16), v) for p, (_, v) in zip(ps, parts)])
    return o * (1.0 / l)


def _head(ref, g):
    return ref[:, g * HEAD_DIM:(g + 1) * HEAD_DIM]


def _store_gated(o_ref, z_ref, g, o):
    z = _head(z_ref, g).astype(F32)
    o_ref[:, g * HEAD_DIM:(g + 1) * HEAD_DIM] = (o * _silu(z)).astype(BF16)


def _attn_a_kernel(*refs, with_ctx, n_lat, scale):
    if with_ctx:
        (sink_ref, qc_ref, zc_ref, ql_ref, zl_ref, kc_ref, vc_ref, kl_ref, vl_ref, cos_ref, sin_ref,
         oc_ref, ol_ref, kl_s) = refs
    else:
        (sink_ref, ql_ref, zl_ref, kc_ref, vc_ref, kl_ref, vl_ref, cos_ref, sin_ref, ol_ref, kl_s) = refs
    hk = pl.program_id(1)
    qi = pl.program_id(2)
    q0 = 1 if with_ctx else 0
    tq = ATTN_TQ
    win_k = tq + 2 * WINDOW

    @pl.when(qi == 0)
    def _prep():
        def body(i, carry):
            r = pl.multiple_of(i * tq, tq)
            k = kl_ref[pl.ds(r, tq), :].astype(F32)
            kl_s[pl.ds(r, tq), :] = _rope(k, cos_ref[pl.ds(r, tq), :], sin_ref[pl.ds(r, tq), :]).astype(BF16)
            return carry
        lax.fori_loop(0, n_lat // tq, body, 0)

    if with_ctx:
        @pl.when(qi == 0)
        def _ctx():
            for g in range(GQA_GROUP):
                s = _dot_nt(_head(qc_ref, g), kc_ref[...]) * scale
                o = _softmax_pv([(s, vc_ref[...])], sink_ref[hk * GQA_GROUP + g])
                _store_gated(oc_ref, zc_ref, g, o)

    @pl.when(qi >= q0)
    def _lat():
        r0 = pl.multiple_of((qi - q0) * tq, tq)
        ws = pl.multiple_of(jnp.clip(r0 - WINDOW, 0, n_lat - win_k), WINDOW)
        kw = kl_s[pl.ds(ws, win_k), :]
        vw = vl_ref[pl.ds(ws, win_k), :]
        qpos = r0 + lax.broadcasted_iota(jnp.int32, (tq, win_k), 0)
        kpos = ws + lax.broadcasted_iota(jnp.int32, (tq, win_k), 1)
        valid = jnp.abs(qpos - kpos) <= WINDOW
        cos = cos_ref[pl.ds(r0, tq), :]
        sin = sin_ref[pl.ds(r0, tq), :]
        for g in range(GQA_GROUP):
            q = _rope(_head(ql_ref, g).astype(F32), cos, sin).astype(BF16)
            s_w = jnp.where(valid, _dot_nt(q, kw) * scale, NEG_BIG)
            s_c = _dot_nt(q, kc_ref[...]) * scale
            o = _softmax_pv([(s_w, vw), (s_c, vc_ref[...])], sink_ref[hk * GQA_GROUP + g])
            _store_gated(ol_ref, zl_ref, g, o)


def _attn_b_kernel(*refs, with_ctx, n_lat, scale):
    if with_ctx:
        (qc_ref, zc_ref, ql_ref, zl_ref, kc_ref, vc_ref, kl_ref, vl_ref, cos_ref, sin_ref, qn_ref, kn_ref,
         oc_ref, ol_ref, kc_s, kl_s) = refs
    else:
        (ql_ref, zl_ref, kc_ref, vc_ref, kl_ref, vl_ref, cos_ref, sin_ref, qn_ref, kn_ref,
         ol_ref, kc_s, kl_s) = refs
    qi = pl.program_id(2)
    q0 = 1 if with_ctx else 0
    tq = ATTN_TQ

    @pl.when(qi == 0)
    def _prep():
        kn = kn_ref[...] * scale
        kc_s[...] = _rms(kc_ref[...].astype(F32), kn).astype(BF16)

        def body(i, carry):
            r = pl.multiple_of(i * tq, tq)
            k = _rms(kl_ref[pl.ds(r, tq), :].astype(F32), kn)
            kl_s[pl.ds(r, tq), :] = _rope(k, cos_ref[pl.ds(r, tq), :], sin_ref[pl.ds(r, tq), :]).astype(BF16)
            return carry
        lax.fori_loop(0, n_lat // tq, body, 0)

    if with_ctx:
        @pl.when(qi == 0)
        def _ctx():
            for g in range(GQA_GROUP):
                q = _rms(_head(qc_ref, g).astype(F32), qn_ref[...]).astype(BF16)
                o = _softmax_pv([(_dot_nt(q, kc_s[...]), vc_ref[...])], None)
                _store_gated(oc_ref, zc_ref, g, o)

    @pl.when(qi >= q0)
    def _lat():
        r0 = pl.multiple_of((qi - q0) * tq, tq)
        cos = cos_ref[pl.ds(r0, tq), :]
        sin = sin_ref[pl.ds(r0, tq), :]
        for g in range(GQA_GROUP):
            q = _rope(_rms(_head(ql_ref, g).astype(F32), qn_ref[...]), cos, sin).astype(BF16)
            o = _softmax_pv([(_dot_nt(q, kc_s[...]), vc_ref[...]), (_dot_nt(q, kl_s[...]), vl_ref[...])], None)
            _store_gated(ol_ref, zl_ref, g, o)


def _attn_call(kind, p_c, p_l, extras, cos, sin, batch, with_ctx):
    m = p_c.shape[0] // batch
    n = p_l.shape[0] // batch
    tq = ATTN_TQ
    assert m == tq and n % tq == 0 and n >= tq + 2 * WINDOW
    nq = n // tq
    q0 = 1 if with_ctx else 0
    if kind == "a":
        qcol, kcol, vcol, zcol = A_Q, A_K, A_V, A_Z
    else:
        qcol, kcol, vcol, zcol = B_Q, B_K, B_V, B_Z
    gw, hd = GROUP_WIDTH, HEAD_DIM

    def lat_row(b, hk, qi):
        return b * nq + jnp.maximum(qi - q0, 0)

    def grp_c(col):
        return pl.BlockSpec((tq, gw), lambda b, hk, qi: (b, col // gw + hk))

    def grp_l(col):
        return pl.BlockSpec((tq, gw), lambda b, hk, qi: (lat_row(b, hk, qi), col // gw + hk))

    def kv(rows, col):
        return pl.BlockSpec((rows, hd), lambda b, hk, qi: (b, col // hd + hk))

    def full(a):
        return pl.BlockSpec(a.shape, lambda b, hk, qi: (0,) * a.ndim)

    in_specs, args = [], []
    if kind == "a":
        in_specs.append(pl.BlockSpec(memory_space=pltpu.SMEM))
        args.append(extras[0])
    if with_ctx:
        in_specs += [grp_c(qcol), grp_c(zcol)]
        args += [p_c, p_c]
    in_specs += [grp_l(qcol), grp_l(zcol), kv(m, kcol), kv(m, vcol), kv(n, kcol), kv(n, vcol), full(cos), full(sin)]
    args += [p_l, p_l, p_c, p_c, p_l, p_l, cos, sin]
    scratch = [pltpu.VMEM((n, hd), BF16)]
    if kind == "b":
        in_specs += [full(extras[0]), full(extras[1])]
        args += list(extras)
        scratch = [pltpu.VMEM((m, hd), BF16)] + scratch

    out_l_spec = pl.BlockSpec((tq, gw), lambda b, hk, qi: (lat_row(b, hk, qi), hk))
    out_l_shape = jax.ShapeDtypeStruct((batch * n, Q_WIDTH), BF16)
    if with_ctx:
        out_specs = [pl.BlockSpec((tq, gw), lambda b, hk, qi: (b, hk)), out_l_spec]
        out_shape = [jax.ShapeDtypeStruct((batch * m, Q_WIDTH), BF16), out_l_shape]
    else:
        out_specs, out_shape = [out_l_spec], [out_l_shape]

    body = _attn_a_kernel if kind == "a" else _attn_b_kernel
    outs = pl.pallas_call(
        functools.partial(body, with_ctx=with_ctx, n_lat=n, scale=HEAD_DIM ** -0.5),
        grid=(batch, A_KV_HEADS, nq + q0),
        in_specs=in_specs,
        out_specs=out_specs,
        out_shape=out_shape,
        scratch_shapes=scratch,
        compiler_params=_params("parallel", "parallel", "arbitrary"),
        name="attn_" + kind,
    )(*args)
    return (outs[0], outs[1]) if with_ctx else (None, outs[0])


HGRN_PAIRS = ((0, 32), (1, 2), (4, 8), (16, None))
HGRN_SELECT_LEVELS = (1, 2, 4, 8)
HGRN_SIGN_LEVELS = (1, 2, 4)
LOG2E = 1.4426950408889634


def _hgrn_constants():
    c = HGRN_CHUNK
    t = np.arange(c)[:, None]
    s = np.arange(c)[None, :]
    tri = s <= t

    def pair_mask(h):
        if h is None:
            return np.zeros((c, c), bool)
        if h == 0:
            return t == s
        blk = 2 * h
        return (t // blk == s // blk) & (t % blk >= h) & (s % blk < h)

    mw = np.stack([np.stack([np.concatenate([pair_mask(a).T if rev else pair_mask(a),
                                             pair_mask(b).T if rev else pair_mask(b)], axis=1)
                             for a, b in HGRN_PAIRS]) for rev in (False, True)])
    rows = np.broadcast_to(np.arange(c)[:, None], (c, HEAD_DIM))

    def t_role(h, rev):
        ph = rows % (2 * h)
        return (ph < h) if rev else (ph >= h)

    role = np.stack([np.stack([t_role(h, rev) for h in HGRN_SELECT_LEVELS]) for rev in (False, True)])
    sgn = np.stack([np.stack([np.where(t_role(h, rev), 1.0, -1.0) for h in HGRN_SIGN_LEVELS])
                    for rev in (False, True)])
    return (jnp.asarray(np.stack([tri, tri.T]), dtype=BF16), jnp.asarray(mw, dtype=BF16),
            jnp.asarray(role, dtype=BF16), jnp.asarray(sgn, dtype=F32))


def _hgrn_gates(z, lbp, tri):
    log_lb, log1m_lb, one_m_lb = lbp
    u = jnp.exp(-jnp.abs(z))
    log_sig = jnp.minimum(z, 0.0) - jnp.log(1.0 + u)
    b = log1m_lb + log_sig
    g = jnp.maximum(log_lb, b) + jnp.log(1.0 + jnp.exp(-jnp.abs(log_lb - b)))
    kk = one_m_lb * jnp.where(z >= 0.0, u, 1.0) * (1.0 / (1.0 + u))
    g_hi = g.astype(BF16)
    g_lo = (g - g_hi.astype(F32)).astype(BF16)
    cc = _dot(tri, jnp.concatenate([g_hi, g_lo], axis=1))
    return kk.astype(BF16), (cc[:, :HEAD_DIM] + cc[:, HEAD_DIM:]) * LOG2E


def _hgrn_state_operands(q, kk, cum_ref, r, rev):
    c = HGRN_CHUNK
    cum = cum_ref[pl.ds(r, c), :]
    tot = cum_ref[pl.ds(r + (0 if rev else c - 1), 1), :]
    return q * jnp.exp2(cum).astype(BF16), kk * jnp.exp2(tot - cum).astype(BF16), tot


def _hgrn_pair_operands(q, kk, cum_ref, r, role_ref, sgn_ref, rev):
    c = HGRN_CHUNK
    cum = cum_ref[pl.ds(r, c), :]

    def row(k):
        return cum_ref[pl.ds(r + k, 1), :]

    sub = lax.broadcasted_iota(jnp.int32, (8, HEAD_DIM), 0)

    def operand(h):
        blk = 2 * h
        if h >= 8:
            d_parts, o_parts = [], []
            for b in range(c // blk):
                lo, mid, hi = b * blk, b * blk + h, (b + 1) * blk
                ref = row(mid if rev else mid - 1)
                if rev:
                    d_parts += [cum[lo:mid] - ref, ref - cum[mid:hi]]
                    o_parts += [q[lo:mid], kk[mid:hi]]
                else:
                    d_parts += [ref - cum[lo:mid], cum[mid:hi] - ref]
                    o_parts += [kk[lo:mid], q[mid:hi]]
            d = jnp.concatenate(d_parts, axis=0)
            if h >= 16:
                return jnp.concatenate(o_parts, axis=0) * jnp.exp2(d).astype(BF16)
        else:
            parts = []
            for grp in range(c // 8):
                base = grp * 8
                if h == 4:
                    ref = row(base + (4 if rev else 3))
                elif h == 2:
                    ref = jnp.where(sub < 4, row(base + (2 if rev else 1)), row(base + (6 if rev else 5)))
                else:
                    k0 = base + (1 if rev else 0)
                    ref = jnp.where(sub < 2, row(k0), jnp.where(sub < 4, row(k0 + 2),
                                                               jnp.where(sub < 6, row(k0 + 4), row(k0 + 6))))
                parts.append(cum[base:base + 8] - ref)
            d = jnp.concatenate(parts, axis=0) * sgn_ref[HGRN_SIGN_LEVELS.index(h)]
        opnd = jnp.where(role_ref[HGRN_SELECT_LEVELS.index(h)] > 0, q, kk)
        return opnd * jnp.exp2(d).astype(BF16)

    def sides(item):
        if item == 0:
            return q, kk
        z = operand(item)
        return z, z

    zeros = jnp.zeros((c, HEAD_DIM), BF16)
    pairs = []
    for a, b in HGRN_PAIRS:
        la, ra = sides(a)
        if b is None:
            pairs.append((la, jnp.concatenate([ra, zeros], axis=0)))
        else:
            lb, rb = sides(b)
            pairs.append((jnp.concatenate([la, lb], axis=1),
                          jnp.concatenate([jnp.concatenate([ra, zeros], axis=1),
                                           jnp.concatenate([zeros, rb], axis=1)], axis=0)))
    return pairs


def _hgrn_blocks(x):
    return x[:HGRN_CHUNK, :HEAD_DIM], x[HGRN_CHUNK:, HEAD_DIM:]


def _hgrn_intra_products(pairs_f, pairs_b):
    out = [_hgrn_blocks(_dot_nt(jnp.concatenate([pf[0], pb[0]], axis=0), jnp.concatenate([pf[1], pb[1]], axis=0)))
           for pf, pb in zip(pairs_f, pairs_b)]
    return [o[0] for o in out], [o[1] for o in out]


def _hgrn_intra_attn(products, mw_ref, d):
    acc = None
    for j, t in enumerate(products):
        t = t.astype(BF16) * mw_ref[d, j]
        acc = t if acc is None else acc + t
    return acc


def _hgrn_outputs(fwd, bwd):
    (qf, kf, tf), af, vf, sf = fwd
    (qb, kb, tb), ab, vb, sb = bwd
    st_f, st_b = sf[...], sb[...]
    o_f, o_b = _hgrn_blocks(_dot_nt(jnp.concatenate([qf, qb], axis=0),
                                    jnp.concatenate([st_f.astype(BF16), st_b.astype(BF16)], axis=0)))
    sf[...] = st_f * jnp.exp2(tf) + _dot_tn(vf, kf)
    sb[...] = st_b * jnp.exp2(tb) + _dot_tn(vb, kb)
    i_f, i_b = _hgrn_blocks(_dot(jnp.concatenate([af, ab], axis=0),
                                 jnp.concatenate([jnp.concatenate([vf, vf], axis=0),
                                                  jnp.concatenate([vb, vb], axis=0)], axis=1)))
    return o_f + i_f, o_b + i_b


def _hgrn_kernel(*refs, with_ctx, m_ctx, n_lat):
    (qc_ref, fc_ref, bc_ref, ic_ref, ql_ref, fl_ref, bl_ref, il_ref) = refs[:8]
    refs = refs[8:]
    if with_ctx:
        zc_ref, refs = refs[0], refs[1:]
    zl_ref, lb_ref, gain_ref, tri_ref, mw_ref, role_ref, sgn_ref = refs[:7]
    refs = refs[7:]
    if with_ctx:
        oc_ref, refs = refs[0], refs[1:]
    ol_ref = refs[0]
    st_f, st_b, q_s, v_s, kkf_s, kkb_s, cumf_s, cumb_s, of_s, ob_s = refs[1:]
    chunk = HGRN_CHUNK
    nc_c, nc_l = m_ctx // chunk, n_lat // chunk

    def lb_params(d):
        lb = lb_ref[pl.ds(d, 1), :]
        return jnp.log(lb), jnp.log(1.0 - lb), 1.0 - lb

    lbp_f, lbp_b = lb_params(0), lb_params(1)

    def gates(q_ref, zf_ref, zb_ref, i_ref, n_chunks, base):
        def body(c, carry):
            r = pl.multiple_of(c * chunk, chunk)
            dst = pl.ds(pl.multiple_of(base + r, chunk), chunk)
            q_s[dst, :] = q_ref[pl.ds(r, chunk), :]
            v_s[dst, :] = i_ref[pl.ds(r, chunk), :]
            kkf_s[dst, :], cumf_s[dst, :] = _hgrn_gates(zf_ref[pl.ds(r, chunk), :].astype(F32), lbp_f, tri_ref[0])
            kkb_s[dst, :], cumb_s[dst, :] = _hgrn_gates(zb_ref[pl.ds(r, chunk), :].astype(F32), lbp_b, tri_ref[1])
            return carry
        lax.fori_loop(0, n_chunks, body, 0, unroll=4)

    gates(qc_ref, fc_ref, bc_ref, ic_ref, nc_c, 0)
    gates(ql_ref, fl_ref, bl_ref, il_ref, nc_l, m_ctx)

    st_f[...] = jnp.zeros_like(st_f)
    st_b[...] = jnp.zeros_like(st_b)
    n_chunks = nc_c + nc_l

    def rows(c):
        cb = jnp.where(c < nc_c, nc_c - 1 - c, n_chunks - 1 - (c - nc_c))
        return pl.multiple_of(c * chunk, chunk), pl.multiple_of(cb * chunk, chunk)

    def products(c):
        rf, rb = rows(c)
        return _hgrn_intra_products(
            _hgrn_pair_operands(q_s[pl.ds(rf, chunk), :], kkf_s[pl.ds(rf, chunk), :], cumf_s, rf,
                                role_ref.at[0], sgn_ref.at[0], False),
            _hgrn_pair_operands(q_s[pl.ds(rb, chunk), :], kkb_s[pl.ds(rb, chunk), :], cumb_s, rb,
                                role_ref.at[1], sgn_ref.at[1], True))

    def attn(prods):
        return _hgrn_intra_attn(prods[0], mw_ref, 0), _hgrn_intra_attn(prods[1], mw_ref, 1)

    def body(c, carry):
        attn_f, attn_b = carry
        nxt = products(jnp.minimum(c + 1, n_chunks - 1))
        rf, rb = rows(c)
        of_s[pl.ds(rf, chunk), :], ob_s[pl.ds(rb, chunk), :] = _hgrn_outputs(
            (_hgrn_state_operands(q_s[pl.ds(rf, chunk), :], kkf_s[pl.ds(rf, chunk), :], cumf_s, rf, False),
             attn_f, v_s[pl.ds(rf, chunk), :], st_f),
            (_hgrn_state_operands(q_s[pl.ds(rb, chunk), :], kkb_s[pl.ds(rb, chunk), :], cumb_s, rb, True),
             attn_b, v_s[pl.ds(rb, chunk), :], st_b))
        return attn(nxt)
    lax.fori_loop(0, n_chunks, body, attn(products(0)),
                  unroll=HGRN_UNROLL if n_chunks % HGRN_UNROLL == 0 else 1)

    def finish(z_ref, o_ref, rows, base):
        blk = 256

        def body(i, carry):
            r = pl.multiple_of(i * blk, blk)
            src = pl.ds(pl.multiple_of(base + r, blk), blk)
            o = _rms((of_s[src, :] + ob_s[src, :]) * (HEAD_DIM ** -0.5), gain_ref[...])
            o_ref[pl.ds(r, blk), :] = (o * _silu(z_ref[pl.ds(r, blk), :].astype(F32))).astype(BF16)
            return carry
        lax.fori_loop(0, rows // blk, body, 0)

    if with_ctx:
        finish(zc_ref, oc_ref, m_ctx, 0)
    finish(zl_ref, ol_ref, n_lat, m_ctx)


def _hgrn_call(p_c, p_l, lb, gain, batch, with_ctx):
    m = p_c.shape[0] // batch
    n = p_l.shape[0] // batch
    hd = HEAD_DIM
    assert m % 256 == 0 and n % 256 == 0
    consts = _hgrn_constants()

    def col(rows, c):
        return pl.BlockSpec((rows, hd), lambda b, h: (b, c // hd + h))

    def full(a):
        return pl.BlockSpec(a.shape, lambda b, h: (0,) * a.ndim)

    in_specs = [col(m, C_Q), col(m, C_ZF), col(m, C_ZB), col(m, C_I),
                col(n, C_Q), col(n, C_ZF), col(n, C_ZB), col(n, C_I)]
    args = [p_c] * 4 + [p_l] * 4
    if with_ctx:
        in_specs.append(col(m, C_Z))
        args.append(p_c)
    in_specs += [col(n, C_Z), pl.BlockSpec((2, hd), lambda b, h: (0, h)), full(gain)] + [full(a) for a in consts]
    args += [p_l, lb, gain, *consts]

    out_l_spec = pl.BlockSpec((n, hd), lambda b, h: (b, h))
    out_l_shape = jax.ShapeDtypeStruct((batch * n, Q_WIDTH), BF16)
    if with_ctx:
        out_specs = [pl.BlockSpec((m, hd), lambda b, h: (b, h)), out_l_spec]
        out_shape = [jax.ShapeDtypeStruct((batch * m, Q_WIDTH), BF16), out_l_shape]
    else:
        out_specs, out_shape = [out_l_spec], [out_l_shape]

    outs = pl.pallas_call(
        functools.partial(_hgrn_kernel, with_ctx=with_ctx, m_ctx=m, n_lat=n),
        grid=(batch, C_HEADS),
        in_specs=in_specs,
        out_specs=out_specs,
        out_shape=out_shape,
        scratch_shapes=([pltpu.VMEM((hd, hd), F32)] * 2 + [pltpu.VMEM((m + n, hd), BF16)] * 4
                        + [pltpu.VMEM((m + n, hd), F32)] * 4),
        compiler_params=_params("parallel", "parallel"),
        name="hgrn2",
    )(*args)
    return (outs[0], outs[1]) if with_ctx else (None, outs[0])


def _merge_kernel(*refs, final_norm):
    (x_ref, ga_ref, gb_ref, gc_ref, ka_ref, kb_ref, kc_ref, gt_ref, wa_ref, wb_ref, wc_ref, wo_ref) = refs[:12]
    o_ref = refs[-1]
    mix = jax.nn.sigmoid(ka_ref[...].astype(F32)) * _dot(ga_ref[...], wa_ref[...])
    mix = mix + jax.nn.sigmoid(kb_ref[...].astype(F32)) * _dot(gb_ref[...], wb_ref[...])
    mix = mix + jax.nn.sigmoid(kc_ref[...].astype(F32)) * _dot(gc_ref[...], wc_ref[...])
    new = x_ref[...] + gt_ref[...] * _dot(mix.astype(BF16), wo_ref[...])
    if final_norm:
        new = _rms(new, refs[12][...])
    o_ref[...] = new


def _merge_call(x, ga, gb, gc, p, mod, wa, wb, wc, wo, rows_per_mod, mod_row0, final_g=None):
    rows, d = x.shape
    tm = _row_tile(rows, rows_per_mod, 256)

    def rowblk(width, colblk=0):
        return pl.BlockSpec((tm, width), lambda i: (i, colblk))

    def weight(w):
        return pl.BlockSpec(w.shape, lambda i: (0, 0), pipeline_mode=pl.Buffered(1))

    in_specs = [rowblk(d), rowblk(Q_WIDTH), rowblk(Q_WIDTH), rowblk(Q_WIDTH),
                rowblk(d, GATES // d), rowblk(d, GATES // d + 1), rowblk(d, GATES // d + 2),
                pl.BlockSpec((None, None, 1, d), lambda i: (mod_row0 + (i * tm) // rows_per_mod, 2, 0, 0)),
                weight(wa), weight(wb), weight(wc), weight(wo)]
    args = [x, ga, gb, gc, p, p, p, mod, wa, wb, wc, wo]
    if final_g is not None:
        in_specs.append(pl.BlockSpec((1, d), lambda i: (0, 0)))
        args.append(final_g.reshape(1, d))
    return pl.pallas_call(
        functools.partial(_merge_kernel, final_norm=final_g is not None),
        grid=(rows // tm,),
        in_specs=in_specs,
        out_specs=rowblk(d),
        out_shape=jax.ShapeDtypeStruct((rows, d), F32),
        compiler_params=_params("parallel"),
        name="merge_out",
    )(*args)


def _rope_tables(n):
    rows = n // GRID_W
    row = jnp.repeat(jnp.arange(rows, dtype=F32), GRID_W)
    col = jnp.tile(jnp.arange(GRID_W, dtype=F32), rows)
    n_freq = HEAD_DIM // 4
    inv_freq = ROPE_THETA ** (-jnp.arange(n_freq, dtype=F32) / n_freq)
    ang = jnp.concatenate([row[:, None] * inv_freq, col[:, None] * inv_freq], axis=-1)
    ang = jnp.concatenate([ang, ang], axis=-1)
    sign = jnp.where(jnp.arange(HEAD_DIM) < HEAD_DIM // 2, -1.0, 1.0).astype(F32)
    return jnp.cos(ang), jnp.sin(ang) * sign


def kernel(x, c, ctx, c_ctx, ada_w, ada_b, norm_g, w_in, a_sink, b_q_norm, b_k_norm, c_lower_bound, c_out_norm,
           w_branch_a, w_branch_b, w_branch_c, w_out, final_norm_g):
    batch, n, d = x.shape
    m = ctx.shape[1]
    depth = ada_w.shape[0]
    assert batch + 1 <= MOD_ROWS
    cos, sin = _rope_tables(n)
    lb_all = jnp.cumsum(jax.nn.softmax(c_lower_bound.astype(F32), axis=0), axis=0)
    lb_all = lb_all - lb_all[0:1]

    cv = jnp.zeros((MOD_ROWS, d), F32).at[:batch].set(c).at[batch].set(c_ctx)
    mod = _ada_call(cv, ada_w, ada_b).reshape(depth, MOD_ROWS, 3, 1, d)

    x_l = x.reshape(batch * n, d)
    x_c = ctx.reshape(batch * m, d)
    for l in range(depth):
        with_ctx = l < depth - 1
        last = l == depth - 1
        w_l = w_in[l].astype(BF16)
        p_c = _inproj_call(x_c, norm_g[l], mod[l], w_l, batch * m, batch)
        p_l = _inproj_call(x_l, norm_g[l], mod[l], w_l, n, 0)
        ga_c, ga_l = _attn_call("a", p_c, p_l, (a_sink[l],), cos, sin, batch, with_ctx)
        gb_c, gb_l = _attn_call("b", p_c, p_l, (b_q_norm[l].reshape(1, -1), b_k_norm[l].reshape(1, -1)),
                                cos, sin, batch, with_ctx)
        gc_c, gc_l = _hgrn_call(p_c, p_l, lb_all[l], c_out_norm[l].reshape(1, -1), batch, with_ctx)
        ws = (w_branch_a[l].astype(BF16), w_branch_b[l].astype(BF16), w_branch_c[l].astype(BF16),
              w_out[l].astype(BF16))
        if with_ctx:
            x_c = _merge_call(x_c, ga_c, gb_c, gc_c, p_c, mod[l], *ws, batch * m, batch)
        x_l = _merge_call(x_l, ga_l, gb_l, gc_l, p_l, mod[l], *ws, n, 0,
                          final_g=final_norm_g if last else None)
    return x_l.reshape(batch, n, d)
```

```python
import functools

import numpy as np
import jax
import jax.numpy as jnp
from jax import lax
from jax.experimental import pallas as pl
from jax.experimental.pallas import tpu as pltpu

F32 = jnp.float32
BF16 = jnp.bfloat16

HEAD_DIM = 128
A_HEADS, A_KV_HEADS = 8, 2
B_HEADS, B_KV_HEADS = 8, 2
C_HEADS = 8
WINDOW = 128
GRID_W = 64
ROPE_THETA = 10000.0
NORM_EPS = 1e-6
N_BRANCH = 3
GQA_GROUP = A_HEADS // A_KV_HEADS
Q_WIDTH = A_HEADS * HEAD_DIM
KV_WIDTH = A_KV_HEADS * HEAD_DIM
GROUP_WIDTH = GQA_GROUP * HEAD_DIM

A_Q, A_K, A_V, A_Z = 0, 1024, 1280, 1536
B_Q, B_K, B_V, B_Z = 2560, 3584, 3840, 4096
C_Q, C_ZF, C_ZB, C_I, C_Z = 5120, 6144, 7168, 8192, 9216
GATES = 10240
IN_PROJ_TN = 1024
LAST_LAYER_CTX_TILES = tuple(sorted({c // IN_PROJ_TN for c in (A_K, A_V, B_K, B_V, C_Q, C_ZF, C_ZB, C_I)}))

ATTN_TQ = 256
HGRN_CHUNK = 64
HGRN_UNROLL = 4
MOD_ROWS = 16
NEG_BIG = -1e30
VMEM_LIMIT = 56 * 1024 * 1024


def _dot(a, b):
    return jnp.dot(a, b, preferred_element_type=F32)


def _dot_nt(a, b):
    return lax.dot_general(a, b, (((1,), (1,)), ((), ())), preferred_element_type=F32)


def _dot_tn(a, b):
    return lax.dot_general(a, b, (((0,), (0,)), ((), ())), preferred_element_type=F32)


def _rms(x, g):
    return x * lax.rsqrt(jnp.mean(x * x, axis=-1, keepdims=True) + NORM_EPS) * g


def _rope(x, cos, sin_signed):
    return x * cos + pltpu.roll(x, HEAD_DIM // 2, 1) * sin_signed


def _silu(x):
    return x * jax.nn.sigmoid(x)


def _params(*sem, flags=None):
    return pltpu.CompilerParams(dimension_semantics=sem, vmem_limit_bytes=VMEM_LIMIT, flags=flags)


def _ada_kernel(cv_ref, w_ref, b_ref, o_ref):
    s = _silu(cv_ref[...]).astype(BF16)
    o_ref[...] = _dot(s, w_ref[...].astype(BF16)) + b_ref[...]


def _ada_call(cv, ada_w, ada_b, tn=512):
    depth, d, n3 = ada_w.shape
    return pl.pallas_call(
        _ada_kernel,
        grid=(depth, n3 // tn),
        in_specs=[
            pl.BlockSpec((MOD_ROWS, d), lambda l, j: (0, 0)),
            pl.BlockSpec((None, d, tn), lambda l, j: (l, 0, j)),
            pl.BlockSpec((None, 1, tn), lambda l, j: (l, 0, j)),
        ],
        out_specs=pl.BlockSpec((None, MOD_ROWS, tn), lambda l, j: (l, 0, j)),
        out_shape=jax.ShapeDtypeStruct((depth, MOD_ROWS, n3), F32),
        compiler_params=_params("parallel", "parallel"),
        name="ada_mod",
    )(cv, ada_w, ada_b.reshape(depth, 1, n3))


def _inproj_kernel(x_ref, g_ref, sc_ref, sh_ref, w_ref, o_ref, h_ref):
    @pl.when(pl.program_id(1) == 0)
    def _():
        h = _rms(x_ref[...], g_ref[...]) * (1.0 + sc_ref[...]) + sh_ref[...]
        h_ref[...] = h.astype(BF16)

    o_ref[...] = _dot(h_ref[...], w_ref[...].astype(BF16)).astype(BF16)


def _row_tile(rows, rows_per_mod, cap):
    t = cap
    while rows % t or rows_per_mod % t:
        t //= 2
    return t


def _inproj_call(x, g, mod, w_all, layer, rows_per_mod, mod_row0, col_tiles=None, tn=1024):
    rows, d = x.shape
    n = w_all.shape[2]
    tm = _row_tile(rows, rows_per_mod, 1024)
    tiles = tuple(range(n // tn)) if col_tiles is None else tuple(col_tiles)

    def col(j):
        c = tiles[-1]
        for k in range(len(tiles) - 2, -1, -1):
            c = jnp.where(j == k, tiles[k], c)
        return c if col_tiles is not None else j

    def mod_spec(which):
        return pl.BlockSpec((None, None, 1, d),
                            lambda i, j: (mod_row0 + (i * tm) // rows_per_mod, which, 0, 0))

    return pl.pallas_call(
        _inproj_kernel,
        grid=(rows // tm, len(tiles)),
        in_specs=[
            pl.BlockSpec((tm, d), lambda i, j: (i, 0)),
            pl.BlockSpec((1, d), lambda i, j: (0, 0)),
            mod_spec(1),
            mod_spec(0),
            pl.BlockSpec((None, d, tn), lambda i, j: (layer, 0, col(j))),
        ],
        out_specs=pl.BlockSpec((tm, tn), lambda i, j: (i, col(j))),
        out_shape=jax.ShapeDtypeStruct((rows, n), BF16),
        scratch_shapes=[pltpu.VMEM((tm, d), BF16)],
        compiler_params=_params("parallel", "arbitrary"),
        name="in_proj",
    )(x, g.reshape(1, d), mod, mod, w_all)


def _softmax_pv(parts, extra_logit):
    m = functools.reduce(jnp.maximum, [jnp.max(s, axis=-1, keepdims=True) for s, _ in parts])
    if extra_logit is not None:
        m = jnp.maximum(m, extra_logit)
    ps = [jnp.exp(s - m) for s, _ in parts]
    l = functools.reduce(jnp.add, [jnp.sum(p, axis=-1, keepdims=True) for p in ps])
    if extra_logit is not None:
        l = l + jnp.exp(extra_logit - m)
    o = functools.reduce(jnp.add, [_dot(p.astype(BF16), v) for p, (_, v) in zip(ps, parts)])
    return o * (1.0 / l)


def _head(ref, g):
    return ref[:, g * HEAD_DIM:(g + 1) * HEAD_DIM]


def _store_gated(o_ref, z_ref, g, o):
    z = _head(z_ref, g).astype(F32)
    o_ref[:, g * HEAD_DIM:(g + 1) * HEAD_DIM] = (o * _silu(z)).astype(BF16)


def _attn_a_kernel(*refs, with_ctx, n_lat, scale):
    if with_ctx:
        (sink_ref, qc_ref, zc_ref, ql_ref, zl_ref, kc_ref, vc_ref, kl_ref, vl_ref, cos_ref, sin_ref,
         oc_ref, ol_ref, kl_s) = refs
    else:
        (sink_ref, ql_ref, zl_ref, kc_ref, vc_ref, kl_ref, vl_ref, cos_ref, sin_ref, ol_ref, kl_s) = refs
    hk = pl.program_id(1)
    qi = pl.program_id(2)
    q0 = 1 if with_ctx else 0
    tq = ATTN_TQ
    win_k = tq + 2 * WINDOW

    @pl.when(qi == 0)
    def _prep():
        def body(i, carry):
            r = pl.multiple_of(i * tq, tq)
            k = kl_ref[pl.ds(r, tq), :].astype(F32)
            kl_s[pl.ds(r, tq), :] = _rope(k, cos_ref[pl.ds(r, tq), :], sin_ref[pl.ds(r, tq), :]).astype(BF16)
            return carry
        lax.fori_loop(0, n_lat // tq, body, 0)

    if with_ctx:
        @pl.when(qi == 0)
        def _ctx():
            for g in range(GQA_GROUP):
                s = _dot_nt(_head(qc_ref, g), kc_ref[...]) * scale
                o = _softmax_pv([(s, vc_ref[...])], sink_ref[hk * GQA_GROUP + g])
                _store_gated(oc_ref, zc_ref, g, o)

    @pl.when(qi >= q0)
    def _lat():
        r0 = pl.multiple_of((qi - q0) * tq, tq)
        ws = pl.multiple_of(jnp.clip(r0 - WINDOW, 0, n_lat - win_k), WINDOW)
        kw = kl_s[pl.ds(ws, win_k), :]
        vw = vl_ref[pl.ds(ws, win_k), :]
        qpos = r0 + lax.broadcasted_iota(jnp.int32, (tq, win_k), 0)
        kpos = ws + lax.broadcasted_iota(jnp.int32, (tq, win_k), 1)
        valid = jnp.abs(qpos - kpos) <= WINDOW
        cos = cos_ref[pl.ds(r0, tq), :]
        sin = sin_ref[pl.ds(r0, tq), :]
        qs = [_rope(_head(ql_ref, g).astype(F32), cos, sin).astype(BF16) for g in range(GQA_GROUP)]
        scores = [(_dot_nt(q, kw), _dot_nt(q, kc_ref[...])) for q in qs]
        for g, (s_w, s_c) in enumerate(scores):
            s_w = jnp.where(valid, s_w * scale, NEG_BIG)
            o = _softmax_pv([(s_w, vw), (s_c * scale, vc_ref[...])], sink_ref[hk * GQA_GROUP + g])
            _store_gated(ol_ref, zl_ref, g, o)


def _attn_b_kernel(*refs, with_ctx, n_lat, scale):
    if with_ctx:
        (qc_ref, zc_ref, ql_ref, zl_ref, kc_ref, vc_ref, kl_ref, vl_ref, cos_ref, sin_ref, qn_ref, kn_ref,
         oc_ref, ol_ref, kc_s, kl_s) = refs
    else:
        (ql_ref, zl_ref, kc_ref, vc_ref, kl_ref, vl_ref, cos_ref, sin_ref, qn_ref, kn_ref,
         ol_ref, kc_s, kl_s) = refs
    qi = pl.program_id(2)
    q0 = 1 if with_ctx else 0
    tq = ATTN_TQ

    @pl.when(qi == 0)
    def _prep():
        kn = kn_ref[...] * scale
        kc_s[...] = _rms(kc_ref[...].astype(F32), kn).astype(BF16)

        def body(i, carry):
            r = pl.multiple_of(i * tq, tq)
            k = _rms(kl_ref[pl.ds(r, tq), :].astype(F32), kn)
            kl_s[pl.ds(r, tq), :] = _rope(k, cos_ref[pl.ds(r, tq), :], sin_ref[pl.ds(r, tq), :]).astype(BF16)
            return carry
        lax.fori_loop(0, n_lat // tq, body, 0)

    if with_ctx:
        @pl.when(qi == 0)
        def _ctx():
            for g in range(GQA_GROUP):
                q = _rms(_head(qc_ref, g).astype(F32), qn_ref[...]).astype(BF16)
                o = _softmax_pv([(_dot_nt(q, kc_s[...]), vc_ref[...])], None)
                _store_gated(oc_ref, zc_ref, g, o)

    @pl.when(qi >= q0)
    def _lat():
        r0 = pl.multiple_of((qi - q0) * tq, tq)
        cos = cos_ref[pl.ds(r0, tq), :]
        sin = sin_ref[pl.ds(r0, tq), :]
        qs = [_rope(_rms(_head(ql_ref, g).astype(F32), qn_ref[...]), cos, sin).astype(BF16) for g in range(GQA_GROUP)]
        scores = [(_dot_nt(q, kc_s[...]), _dot_nt(q, kl_s[...])) for q in qs]
        for g, (s_c, s_l) in enumerate(scores):
            o = _softmax_pv([(s_c, vc_ref[...]), (s_l, vl_ref[...])], None)
            _store_gated(ol_ref, zl_ref, g, o)


def _attn_call(kind, p_c, p_l, extras, cos, sin, batch, with_ctx):
    m = p_c.shape[0] // batch
    n = p_l.shape[0] // batch
    tq = ATTN_TQ
    assert m == tq and n % tq == 0 and n >= tq + 2 * WINDOW
    nq = n // tq
    q0 = 1 if with_ctx else 0
    if kind == "a":
        qcol, kcol, vcol, zcol = A_Q, A_K, A_V, A_Z
    else:
        qcol, kcol, vcol, zcol = B_Q, B_K, B_V, B_Z
    gw, hd = GROUP_WIDTH, HEAD_DIM

    def lat_row(b, hk, qi):
        return b * nq + jnp.maximum(qi - q0, 0)

    def grp_c(col):
        return pl.BlockSpec((tq, gw), lambda b, hk, qi: (b, col // gw + hk))

    def grp_l(col):
        return pl.BlockSpec((tq, gw), lambda b, hk, qi: (lat_row(b, hk, qi), col // gw + hk))

    def kv(rows, col):
        return pl.BlockSpec((rows, hd), lambda b, hk, qi: (b, col // hd + hk))

    def full(a):
        return pl.BlockSpec(a.shape, lambda b, hk, qi: (0,) * a.ndim)

    in_specs, args = [], []
    if kind == "a":
        in_specs.append(pl.BlockSpec(memory_space=pltpu.SMEM))
        args.append(extras[0])
    if with_ctx:
        in_specs += [grp_c(qcol), grp_c(zcol)]
        args += [p_c, p_c]
    in_specs += [grp_l(qcol), grp_l(zcol), kv(m, kcol), kv(m, vcol), kv(n, kcol), kv(n, vcol), full(cos), full(sin)]
    args += [p_l, p_l, p_c, p_c, p_l, p_l, cos, sin]
    scratch = [pltpu.VMEM((n, hd), BF16)]
    if kind == "b":
        in_specs += [full(extras[0]), full(extras[1])]
        args += list(extras)
        scratch = [pltpu.VMEM((m, hd), BF16)] + scratch

    out_l_spec = pl.BlockSpec((tq, gw), lambda b, hk, qi: (lat_row(b, hk, qi), hk))
    out_l_shape = jax.ShapeDtypeStruct((batch * n, Q_WIDTH), BF16)
    if with_ctx:
        out_specs = [pl.BlockSpec((tq, gw), lambda b, hk, qi: (b, hk)), out_l_spec]
        out_shape = [jax.ShapeDtypeStruct((batch * m, Q_WIDTH), BF16), out_l_shape]
    else:
        out_specs, out_shape = [out_l_spec], [out_l_shape]

    body = _attn_a_kernel if kind == "a" else _attn_b_kernel
    outs = pl.pallas_call(
        functools.partial(body, with_ctx=with_ctx, n_lat=n, scale=HEAD_DIM ** -0.5),
        grid=(batch, A_KV_HEADS, nq + q0),
        in_specs=in_specs,
        out_specs=out_specs,
        out_shape=out_shape,
        scratch_shapes=scratch,
        compiler_params=_params("parallel", "parallel", "arbitrary"),
        name="attn_" + kind,
    )(*args)
    return (outs[0], outs[1]) if with_ctx else (None, outs[0])


HGRN_PAIRS = ((0, 32), (1, 2), (4, 8), (16, None))
HGRN_SELECT_LEVELS = (1, 2, 4, 8)
HGRN_SIGN_LEVELS = (1, 2, 4)
LOG2E = 1.4426950408889634


def _hgrn_constants():
    c = HGRN_CHUNK
    t = np.arange(c)[:, None]
    s = np.arange(c)[None, :]
    tri = s <= t

    def pair_mask(h):
        if h is None:
            return np.zeros((c, c), bool)
        if h == 0:
            return t == s
        blk = 2 * h
        return (t // blk == s // blk) & (t % blk >= h) & (s % blk < h)

    mw = np.stack([np.stack([np.concatenate([pair_mask(a).T if rev else pair_mask(a),
                                             pair_mask(b).T if rev else pair_mask(b)], axis=1)
                             for a, b in HGRN_PAIRS]) for rev in (False, True)])
    rows = np.broadcast_to(np.arange(c)[:, None], (c, HEAD_DIM))

    def t_role(h, rev):
        ph = rows % (2 * h)
        return (ph < h) if rev else (ph >= h)

    role = np.stack([np.stack([t_role(h, rev) for h in HGRN_SELECT_LEVELS]) for rev in (False, True)])
    sgn = np.stack([np.stack([np.where(t_role(h, rev), 1.0, -1.0) for h in HGRN_SIGN_LEVELS])
                    for rev in (False, True)])
    return (jnp.asarray(np.stack([tri, tri.T]), dtype=BF16), jnp.asarray(mw, dtype=BF16),
            jnp.asarray(role, dtype=BF16), jnp.asarray(sgn, dtype=F32))


def _hgrn_gates(z, lbp, tri):
    log_lb, log1m_lb, one_m_lb = lbp
    u = jnp.exp(-jnp.abs(z))
    log_sig = jnp.minimum(z, 0.0) - jnp.log(1.0 + u)
    b = log1m_lb + log_sig
    g = jnp.maximum(log_lb, b) + jnp.log(1.0 + jnp.exp(-jnp.abs(log_lb - b)))
    kk = one_m_lb * jnp.where(z >= 0.0, u, 1.0) * (1.0 / (1.0 + u))
    g_hi = g.astype(BF16)
    g_lo = (g - g_hi.astype(F32)).astype(BF16)
    cc = _dot(tri, jnp.concatenate([g_hi, g_lo], axis=1))
    return kk.astype(BF16), (cc[:, :HEAD_DIM] + cc[:, HEAD_DIM:]) * LOG2E


def _hgrn_state_operands(q, kk, cum_ref, r, rev):
    c = HGRN_CHUNK
    cum = cum_ref[pl.ds(r, c), :]
    tot = cum_ref[pl.ds(r + (0 if rev else c - 1), 1), :]
    return q * jnp.exp2(cum).astype(BF16), kk * jnp.exp2(tot - cum).astype(BF16), tot


def _hgrn_pair_operands(q, kk, cum_ref, r, role_ref, sgn_ref, rev):
    c = HGRN_CHUNK
    cum = cum_ref[pl.ds(r, c), :]

    def row(k):
        return cum_ref[pl.ds(r + k, 1), :]

    sub = lax.broadcasted_iota(jnp.int32, (8, HEAD_DIM), 0)

    def operand(h):
        blk = 2 * h
        if h >= 8:
            d_parts, o_parts = [], []
            for b in range(c // blk):
                lo, mid, hi = b * blk, b * blk + h, (b + 1) * blk
                ref = row(mid if rev else mid - 1)
                if rev:
                    d_parts += [cum[lo:mid] - ref, ref - cum[mid:hi]]
                    o_parts += [q[lo:mid], kk[mid:hi]]
                else:
                    d_parts += [ref - cum[lo:mid], cum[mid:hi] - ref]
                    o_parts += [kk[lo:mid], q[mid:hi]]
            d = jnp.concatenate(d_parts, axis=0)
            if h >= 16:
                return jnp.concatenate(o_parts, axis=0) * jnp.exp2(d).astype(BF16)
        else:
            parts = []
            for grp in range(c // 8):
                base = grp * 8
                if h == 4:
                    ref = row(base + (4 if rev else 3))
                elif h == 2:
                    ref = jnp.where(sub < 4, row(base + (2 if rev else 1)), row(base + (6 if rev else 5)))
                else:
                    k0 = base + (1 if rev else 0)
                    ref = jnp.where(sub < 2, row(k0), jnp.where(sub < 4, row(k0 + 2),
                                                               jnp.where(sub < 6, row(k0 + 4), row(k0 + 6))))
                parts.append(cum[base:base + 8] - ref)
            d = jnp.concatenate(parts, axis=0) * sgn_ref[HGRN_SIGN_LEVELS.index(h)]
        opnd = jnp.where(role_ref[HGRN_SELECT_LEVELS.index(h)] > 0, q, kk)
        return opnd * jnp.exp2(d).astype(BF16)

    def sides(item):
        if item == 0:
            return q, kk
        z = operand(item)
        return z, z

    zeros = jnp.zeros((c, HEAD_DIM), BF16)
    pairs = []
    for a, b in HGRN_PAIRS:
        la, ra = sides(a)
        if b is None:
            pairs.append((la, jnp.concatenate([ra, zeros], axis=0)))
        else:
            lb, rb = sides(b)
            pairs.append((jnp.concatenate([la, lb], axis=1),
                          jnp.concatenate([jnp.concatenate([ra, zeros], axis=1),
                                           jnp.concatenate([zeros, rb], axis=1)], axis=0)))
    return pairs


def _hgrn_blocks(x):
    return x[:HGRN_CHUNK, :HEAD_DIM], x[HGRN_CHUNK:, HEAD_DIM:]


def _hgrn_intra_products(pairs_f, pairs_b):
    out = [_hgrn_blocks(_dot_nt(jnp.concatenate([pf[0], pb[0]], axis=0), jnp.concatenate([pf[1], pb[1]], axis=0)))
           for pf, pb in zip(pairs_f, pairs_b)]
    return [o[0] for o in out], [o[1] for o in out]


def _hgrn_intra_attn(products, mw_ref, d):
    acc = None
    for j, t in enumerate(products):
        t = t.astype(BF16) * mw_ref[d, j]
        acc = t if acc is None else acc + t
    return acc


def _hgrn_outputs(fwd, bwd):
    (qf, kf, tf), af, vf, sf = fwd
    (qb, kb, tb), ab, vb, sb = bwd
    st_f, st_b = sf[...], sb[...]
    o_f, o_b = _hgrn_blocks(_dot_nt(jnp.concatenate([qf, qb], axis=0),
                                    jnp.concatenate([st_f.astype(BF16), st_b.astype(BF16)], axis=0)))
    sf[...] = st_f * jnp.exp2(tf) + _dot_tn(vf, kf)
    sb[...] = st_b * jnp.exp2(tb) + _dot_tn(vb, kb)
    i_f, i_b = _hgrn_blocks(_dot(jnp.concatenate([af, ab], axis=0),
                                 jnp.concatenate([jnp.concatenate([vf, vf], axis=0),
                                                  jnp.concatenate([vb, vb], axis=0)], axis=1)))
    return o_f + i_f, o_b + i_b


def _hgrn_kernel(*refs, with_ctx, m_ctx, n_lat):
    (qc_ref, fc_ref, bc_ref, ic_ref, ql_ref, fl_ref, bl_ref, il_ref) = refs[:8]
    refs = refs[8:]
    if with_ctx:
        zc_ref, refs = refs[0], refs[1:]
    zl_ref, lb_ref, gain_ref, tri_ref, mw_ref, role_ref, sgn_ref = refs[:7]
    refs = refs[7:]
    if with_ctx:
        oc_ref, refs = refs[0], refs[1:]
    ol_ref = refs[0]
    st_f, st_b, q_s, v_s, kkf_s, kkb_s, cumf_s, cumb_s, of_s, ob_s = refs[1:]
    chunk = HGRN_CHUNK
    nc_c, nc_l = m_ctx // chunk, n_lat // chunk

    def lb_params(d):
        lb = lb_ref[pl.ds(d, 1), :]
        return jnp.log(lb), jnp.log(1.0 - lb), 1.0 - lb

    lbp_f, lbp_b = lb_params(0), lb_params(1)

    def gates(q_ref, zf_ref, zb_ref, i_ref, n_chunks, base):
        def body(c, carry):
            r = pl.multiple_of(c * chunk, chunk)
            dst = pl.ds(pl.multiple_of(base + r, chunk), chunk)
            q_s[dst, :] = q_ref[pl.ds(r, chunk), :]
            v_s[dst, :] = i_ref[pl.ds(r, chunk), :]
            kkf_s[dst, :], cumf_s[dst, :] = _hgrn_gates(zf_ref[pl.ds(r, chunk), :].astype(F32), lbp_f, tri_ref[0])
            kkb_s[dst, :], cumb_s[dst, :] = _hgrn_gates(zb_ref[pl.ds(r, chunk), :].astype(F32), lbp_b, tri_ref[1])
            return carry
        lax.fori_loop(0, n_chunks, body, 0, unroll=4)

    gates(qc_ref, fc_ref, bc_ref, ic_ref, nc_c, 0)
    gates(ql_ref, fl_ref, bl_ref, il_ref, nc_l, m_ctx)

    st_f[...] = jnp.zeros_like(st_f)
    st_b[...] = jnp.zeros_like(st_b)
    n_chunks = nc_c + nc_l

    def rows(c):
        cb = jnp.where(c < nc_c, nc_c - 1 - c, n_chunks - 1 - (c - nc_c))
        return pl.multiple_of(c * chunk, chunk), pl.multiple_of(cb * chunk, chunk)

    def products(c):
        rf, rb = rows(c)
        return _hgrn_intra_products(
            _hgrn_pair_operands(q_s[pl.ds(rf, chunk), :], kkf_s[pl.ds(rf, chunk), :], cumf_s, rf,
                                role_ref.at[0], sgn_ref.at[0], False),
            _hgrn_pair_operands(q_s[pl.ds(rb, chunk), :], kkb_s[pl.ds(rb, chunk), :], cumb_s, rb,
                                role_ref.at[1], sgn_ref.at[1], True))

    def attn(prods):
        return _hgrn_intra_attn(prods[0], mw_ref, 0), _hgrn_intra_attn(prods[1], mw_ref, 1)

    def body(c, carry):
        attn_f, attn_b = carry
        nxt = products(jnp.minimum(c + 1, n_chunks - 1))
        rf, rb = rows(c)
        of_s[pl.ds(rf, chunk), :], ob_s[pl.ds(rb, chunk), :] = _hgrn_outputs(
            (_hgrn_state_operands(q_s[pl.ds(rf, chunk), :], kkf_s[pl.ds(rf, chunk), :], cumf_s, rf, False),
             attn_f, v_s[pl.ds(rf, chunk), :], st_f),
            (_hgrn_state_operands(q_s[pl.ds(rb, chunk), :], kkb_s[pl.ds(rb, chunk), :], cumb_s, rb, True),
             attn_b, v_s[pl.ds(rb, chunk), :], st_b))
        return attn(nxt)
    lax.fori_loop(0, n_chunks, body, attn(products(0)),
                  unroll=HGRN_UNROLL if n_chunks % HGRN_UNROLL == 0 else 1)

    def finish(z_ref, o_ref, rows, base):
        blk = 256

        def body(i, carry):
            r = pl.multiple_of(i * blk, blk)
            src = pl.ds(pl.multiple_of(base + r, blk), blk)
            o = _rms((of_s[src, :] + ob_s[src, :]) * (HEAD_DIM ** -0.5), gain_ref[...])
            o_ref[pl.ds(r, blk), :] = (o * _silu(z_ref[pl.ds(r, blk), :].astype(F32))).astype(BF16)
            return carry
        lax.fori_loop(0, rows // blk, body, 0)

    if with_ctx:
        finish(zc_ref, oc_ref, m_ctx, 0)
    finish(zl_ref, ol_ref, n_lat, m_ctx)


def _hgrn_call(p_c, p_l, lb, gain, batch, with_ctx):
    m = p_c.shape[0] // batch
    n = p_l.shape[0] // batch
    hd = HEAD_DIM
    assert m % 256 == 0 and n % 256 == 0
    consts = _hgrn_constants()

    def col(rows, c):
        return pl.BlockSpec((rows, hd), lambda b, h: (b, c // hd + h))

    def full(a):
        return pl.BlockSpec(a.shape, lambda b, h: (0,) * a.ndim)

    in_specs = [col(m, C_Q), col(m, C_ZF), col(m, C_ZB), col(m, C_I),
                col(n, C_Q), col(n, C_ZF), col(n, C_ZB), col(n, C_I)]
    args = [p_c] * 4 + [p_l] * 4
    if with_ctx:
        in_specs.append(col(m, C_Z))
        args.append(p_c)
    in_specs += [col(n, C_Z), pl.BlockSpec((2, hd), lambda b, h: (0, h)), full(gain)] + [full(a) for a in consts]
    args += [p_l, lb, gain, *consts]

    out_l_spec = pl.BlockSpec((n, hd), lambda b, h: (b, h))
    out_l_shape = jax.ShapeDtypeStruct((batch * n, Q_WIDTH), BF16)
    if with_ctx:
        out_specs = [pl.BlockSpec((m, hd), lambda b, h: (b, h)), out_l_spec]
        out_shape = [jax.ShapeDtypeStruct((batch * m, Q_WIDTH), BF16), out_l_shape]
    else:
        out_specs, out_shape = [out_l_spec], [out_l_shape]

    outs = pl.pallas_call(
        functools.partial(_hgrn_kernel, with_ctx=with_ctx, m_ctx=m, n_lat=n),
        grid=(batch, C_HEADS),
        in_specs=in_specs,
        out_specs=out_specs,
        out_shape=out_shape,
        scratch_shapes=([pltpu.VMEM((hd, hd), F32)] * 2 + [pltpu.VMEM((m + n, hd), BF16)] * 4
                        + [pltpu.VMEM((m + n, hd), F32)] * 4),
        compiler_params=_params("parallel", "parallel"),
        name="hgrn2",
    )(*args)
    return (outs[0], outs[1]) if with_ctx else (None, outs[0])


def _merge_kernel(*refs, final_norm):
    (x_ref, ga_ref, gb_ref, gc_ref, ka_ref, kb_ref, kc_ref, gt_ref, wa_ref, wb_ref, wc_ref, wo_ref) = refs[:12]
    o_ref = refs[-1]
    mix = jax.nn.sigmoid(ka_ref[...].astype(F32)) * _dot(ga_ref[...], wa_ref[...])
    mix = mix + jax.nn.sigmoid(kb_ref[...].astype(F32)) * _dot(gb_ref[...], wb_ref[...])
    mix = mix + jax.nn.sigmoid(kc_ref[...].astype(F32)) * _dot(gc_ref[...], wc_ref[...])
    new = x_ref[...] + gt_ref[...] * _dot(mix.astype(BF16), wo_ref[...])
    if final_norm:
        new = _rms(new, refs[12][...])
    o_ref[...] = new


def _merge_call(x, ga, gb, gc, p, mod, wa, wb, wc, wo, rows_per_mod, mod_row0, final_g=None):
    rows, d = x.shape
    tm = _row_tile(rows, rows_per_mod, 256)

    def rowblk(width, colblk=0):
        return pl.BlockSpec((tm, width), lambda i: (i, colblk))

    def weight(w):
        return pl.BlockSpec(w.shape, lambda i: (0, 0), pipeline_mode=pl.Buffered(1))

    in_specs = [rowblk(d), rowblk(Q_WIDTH), rowblk(Q_WIDTH), rowblk(Q_WIDTH),
                rowblk(d, GATES // d), rowblk(d, GATES // d + 1), rowblk(d, GATES // d + 2),
                pl.BlockSpec((None, None, 1, d), lambda i: (mod_row0 + (i * tm) // rows_per_mod, 2, 0, 0)),
                weight(wa), weight(wb), weight(wc), weight(wo)]
    args = [x, ga, gb, gc, p, p, p, mod, wa, wb, wc, wo]
    if final_g is not None:
        in_specs.append(pl.BlockSpec((1, d), lambda i: (0, 0)))
        args.append(final_g.reshape(1, d))
    return pl.pallas_call(
        functools.partial(_merge_kernel, final_norm=final_g is not None),
        grid=(rows // tm,),
        in_specs=in_specs,
        out_specs=rowblk(d),
        out_shape=jax.ShapeDtypeStruct((rows, d), F32),
        compiler_params=_params("parallel"),
        name="merge_out",
    )(*args)


def _rope_tables(n):
    rows = n // GRID_W
    row = jnp.repeat(jnp.arange(rows, dtype=F32), GRID_W)
    col = jnp.tile(jnp.arange(GRID_W, dtype=F32), rows)
    n_freq = HEAD_DIM // 4
    inv_freq = ROPE_THETA ** (-jnp.arange(n_freq, dtype=F32) / n_freq)
    ang = jnp.concatenate([row[:, None] * inv_freq, col[:, None] * inv_freq], axis=-1)
    ang = jnp.concatenate([ang, ang], axis=-1)
    sign = jnp.where(jnp.arange(HEAD_DIM) < HEAD_DIM // 2, -1.0, 1.0).astype(F32)
    return jnp.cos(ang), jnp.sin(ang) * sign


def kernel(x, c, ctx, c_ctx, ada_w, ada_b, norm_g, w_in, a_sink, b_q_norm, b_k_norm, c_lower_bound, c_out_norm,
           w_branch_a, w_branch_b, w_branch_c, w_out, final_norm_g):
    batch, n, d = x.shape
    m = ctx.shape[1]
    depth = ada_w.shape[0]
    assert batch + 1 <= MOD_ROWS
    cos, sin = _rope_tables(n)
    lb_all = jnp.cumsum(jax.nn.softmax(c_lower_bound.astype(F32), axis=0), axis=0)
    lb_all = lb_all - lb_all[0:1]

    cv = jnp.zeros((MOD_ROWS, d), F32).at[:batch].set(c).at[batch].set(c_ctx)
    mod = _ada_call(cv, ada_w, ada_b).reshape(depth, MOD_ROWS, 3, 1, d)

    x_l = x.reshape(batch * n, d)
    x_c = ctx.reshape(batch * m, d)
    for l in range(depth):
        with_ctx = l < depth - 1
        last = l == depth - 1
        p_c = _inproj_call(x_c, norm_g[l], mod[l], w_in, l, batch * m, batch,
                           col_tiles=None if with_ctx else LAST_LAYER_CTX_TILES)
        p_l = _inproj_call(x_l, norm_g[l], mod[l], w_in, l, n, 0)
        ga_c, ga_l = _attn_call("a", p_c, p_l, (a_sink[l],), cos, sin, batch, with_ctx)
        gb_c, gb_l = _attn_call("b", p_c, p_l, (b_q_norm[l].reshape(1, -1), b_k_norm[l].reshape(1, -1)),
                                cos, sin, batch, with_ctx)
        gc_c, gc_l = _hgrn_call(p_c, p_l, lb_all[l], c_out_norm[l].reshape(1, -1), batch, with_ctx)
        ws = (w_branch_a[l].astype(BF16), w_branch_b[l].astype(BF16), w_branch_c[l].astype(BF16),
              w_out[l].astype(BF16))
        if with_ctx:
            x_c = _merge_call(x_c, ga_c, gb_c, gc_c, p_c, mod[l], *ws, batch * m, batch)
        x_l = _merge_call(x_l, ga_l, gb_l, gc_l, p_l, mod[l], *ws, n, 0,
                          final_g=final_norm_g if last else None)
    return x_l.reshape(batch, n, d)
```

```python
import functools

import numpy as np
import jax
import jax.numpy as jnp
from jax import lax
from jax.experimental import pallas as pl
from jax.experimental.pallas import tpu as pltpu

F32 = jnp.float32
BF16 = jnp.bfloat16

HEAD_DIM = 128
A_HEADS, A_KV_HEADS = 8, 2
B_HEADS, B_KV_HEADS = 8, 2
C_HEADS = 8
WINDOW = 128
GRID_W = 64
ROPE_THETA = 10000.0
NORM_EPS = 1e-6
N_BRANCH = 3
GQA_GROUP = A_HEADS // A_KV_HEADS
Q_WIDTH = A_HEADS * HEAD_DIM
KV_WIDTH = A_KV_HEADS * HEAD_DIM
GROUP_WIDTH = GQA_GROUP * HEAD_DIM

A_Q, A_K, A_V, A_Z = 0, 1024, 1280, 1536
B_Q, B_K, B_V, B_Z = 2560, 3584, 3840, 4096
C_Q, C_ZF, C_ZB, C_I, C_Z = 5120, 6144, 7168, 8192, 9216
GATES = 10240
IN_PROJ_TN = 1024
GATE_SLAB = 256
LAST_LAYER_CTX_TILES = tuple(sorted({c // IN_PROJ_TN for c in (A_K, A_V, B_K, B_V, C_Q, C_ZF, C_ZB, C_I)}))

ATTN_TQ = 256
HGRN_CHUNK = 64
HGRN_UNROLL = 4
MOD_ROWS = 16
NEG_BIG = -1e30
VMEM_LIMIT = 56 * 1024 * 1024


def _dot(a, b):
    return jnp.dot(a, b, preferred_element_type=F32)


def _dot_nt(a, b):
    return lax.dot_general(a, b, (((1,), (1,)), ((), ())), preferred_element_type=F32)


def _dot_tn(a, b):
    return lax.dot_general(a, b, (((0,), (0,)), ((), ())), preferred_element_type=F32)


def _rms(x, g):
    return x * lax.rsqrt(jnp.mean(x * x, axis=-1, keepdims=True) + NORM_EPS) * g


def _rope(x, cos, sin_signed):
    return x * cos + pltpu.roll(x, HEAD_DIM // 2, 1) * sin_signed


def _silu(x):
    return x * jax.nn.sigmoid(x)


def _params(*sem, flags=None):
    return pltpu.CompilerParams(dimension_semantics=sem, vmem_limit_bytes=VMEM_LIMIT, flags=flags)


def _ada_kernel(cv_ref, w_ref, b_ref, o_ref):
    s = _silu(cv_ref[...]).astype(BF16)
    o_ref[...] = _dot(s, w_ref[...].astype(BF16)) + b_ref[...]


def _ada_call(cv, ada_w, ada_b, tn=512):
    depth, d, n3 = ada_w.shape
    return pl.pallas_call(
        _ada_kernel,
        grid=(depth, n3 // tn),
        in_specs=[
            pl.BlockSpec((MOD_ROWS, d), lambda l, j: (0, 0)),
            pl.BlockSpec((None, d, tn), lambda l, j: (l, 0, j)),
            pl.BlockSpec((None, 1, tn), lambda l, j: (l, 0, j)),
        ],
        out_specs=pl.BlockSpec((None, MOD_ROWS, tn), lambda l, j: (l, 0, j)),
        out_shape=jax.ShapeDtypeStruct((depth, MOD_ROWS, n3), F32),
        compiler_params=_params("parallel", "parallel"),
        name="ada_mod",
    )(cv, ada_w, ada_b.reshape(depth, 1, n3))


def _modulated_norm(x, g, sc, sh):
    return _rms(x, g) * (1.0 + sc) + sh


def _forget_gate(z, lb):
    log_lb, log1m_lb, one_m_lb = jnp.log(lb), jnp.log(1.0 - lb), 1.0 - lb
    u = jnp.exp(-jnp.abs(z))
    log_sig = jnp.minimum(z, 0.0) - jnp.log(1.0 + u)
    b = log1m_lb + log_sig
    g = jnp.maximum(log_lb, b) + jnp.log(1.0 + jnp.exp(-jnp.abs(log_lb - b)))
    kk = one_m_lb * jnp.where(z >= 0.0, u, 1.0) * (1.0 / (1.0 + u))
    return g, kk


def _inproj_kernel(*refs, pre_normed, tiles):
    if pre_normed:
        h_ref, w_ref, lb_ref, o_ref, lf_ref = refs
    else:
        x_ref, g_ref, sc_ref, sh_ref, w_ref, lb_ref, o_ref, lf_ref, h_ref = refs

        @pl.when(pl.program_id(1) == 0)
        def _():
            h_ref[...] = _modulated_norm(x_ref[...], g_ref[...], sc_ref[...], sh_ref[...]).astype(BF16)

    tile = pl.program_id(1) if tiles is None else _lookup(tiles, pl.program_id(1))
    is_gate = (tile == C_ZF // IN_PROJ_TN) | (tile == C_ZB // IN_PROJ_TN)

    @pl.when(jnp.logical_not(is_gate))
    def _():
        o_ref[...] = _dot(h_ref[...], w_ref[...].astype(BF16)).astype(BF16)

    @pl.when(is_gate)
    def _():
        acc = _dot(h_ref[...], w_ref[...].astype(BF16))
        for c0 in range(0, IN_PROJ_TN, GATE_SLAB):
            g, kk = _forget_gate(acc[:, c0:c0 + GATE_SLAB], lb_ref[:, c0:c0 + GATE_SLAB])
            o_ref[:, c0:c0 + GATE_SLAB] = kk.astype(BF16)
            lf_ref[:, c0:c0 + GATE_SLAB] = g


def _lookup(table, j):
    v = table[-1]
    for k in range(len(table) - 2, -1, -1):
        v = jnp.where(j == k, table[k], v)
    return v


def _row_tile(rows, rows_per_mod, cap):
    t = cap
    while rows % t or rows_per_mod % t:
        t //= 2
    return t


def _inproj_call(x, norm, w_all, layer, lb, rows_per_mod, mod_row0, col_tiles=None):
    rows, d = x.shape
    n = w_all.shape[2]
    tn = IN_PROJ_TN
    tm = _row_tile(rows, rows_per_mod, 1024)
    tiles = tuple(range(n // tn)) if col_tiles is None else tuple(col_tiles)
    zb_tile = C_ZB // tn

    def col(j):
        return j if col_tiles is None else _lookup(tiles, j)

    def dirn(j):
        return jnp.where(col(j) >= zb_tile, 1, 0)

    in_specs = [pl.BlockSpec((tm, d), lambda i, j: (i, 0))]
    args = [x]
    scratch = []
    if norm is not None:
        g, mod = norm

        def mod_spec(which):
            return pl.BlockSpec((None, None, 1, d),
                                lambda i, j: (mod_row0 + (i * tm) // rows_per_mod, which, 0, 0))

        in_specs += [pl.BlockSpec((1, d), lambda i, j: (0, 0)), mod_spec(1), mod_spec(0)]
        args += [g.reshape(1, d), mod, mod]
        scratch = scratch + [pltpu.VMEM((tm, d), BF16)]
    in_specs += [pl.BlockSpec((None, d, tn), lambda i, j: (layer, 0, col(j))),
                 pl.BlockSpec((None, 1, tn), lambda i, j: (dirn(j), 0, 0))]
    args += [w_all, lb.reshape(2, 1, tn)]
    return pl.pallas_call(
        functools.partial(_inproj_kernel, pre_normed=norm is None, tiles=None if col_tiles is None else tiles),
        grid=(rows // tm, len(tiles)),
        in_specs=in_specs,
        out_specs=[pl.BlockSpec((tm, tn), lambda i, j: (i, col(j))),
                   pl.BlockSpec((tm, tn), lambda i, j: (i, dirn(j)))],
        out_shape=[jax.ShapeDtypeStruct((rows, n), BF16), jax.ShapeDtypeStruct((rows, 2 * tn), F32)],
        scratch_shapes=scratch,
        compiler_params=_params("parallel", "arbitrary"),
        name="in_proj",
    )(*args)


def _softmax_pv(parts, extra_logit):
    m = functools.reduce(jnp.maximum, [jnp.max(s, axis=-1, keepdims=True) for s, _ in parts])
    if extra_logit is not None:
        m = jnp.maximum(m, extra_logit)
    ps = [jnp.exp(s - m) for s, _ in parts]
    l = functools.reduce(jnp.add, [jnp.sum(p, axis=-1, keepdims=True) for p in ps])
    if extra_logit is not None:
        l = l + jnp.exp(extra_logit - m)
    o = functools.reduce(jnp.add, [_dot(p.astype(BF16), v) for p, (_, v) in zip(ps, parts)])
    return o * (1.0 / l)


def _head(ref, g):
    return ref[:, g * HEAD_DIM:(g + 1) * HEAD_DIM]


def _store_gated(o_ref, z_ref, g, o):
    z = _head(z_ref, g).astype(F32)
    o_ref[:, g * HEAD_DIM:(g + 1) * HEAD_DIM] = (o * _silu(z)).astype(BF16)


def _attn_a_kernel(*refs, with_ctx, n_lat, scale):
    if with_ctx:
        (sink_ref, qc_ref, zc_ref, ql_ref, zl_ref, kc_ref, vc_ref, kl_ref, vl_ref, cos_ref, sin_ref,
         oc_ref, ol_ref, kl_s) = refs
    else:
        (sink_ref, ql_ref, zl_ref, kc_ref, vc_ref, kl_ref, vl_ref, cos_ref, sin_ref, ol_ref, kl_s) = refs
    hk = pl.program_id(1)
    qi = pl.program_id(2)
    q0 = 1 if with_ctx else 0
    tq = ATTN_TQ
    win_k = tq + 2 * WINDOW

    @pl.when(qi == 0)
    def _prep():
        def body(i, carry):
            r = pl.multiple_of(i * tq, tq)
            k = kl_ref[pl.ds(r, tq), :].astype(F32)
            kl_s[pl.ds(r, tq), :] = _rope(k, cos_ref[pl.ds(r, tq), :], sin_ref[pl.ds(r, tq), :]).astype(BF16)
            return carry
        lax.fori_loop(0, n_lat // tq, body, 0)

    if with_ctx:
        @pl.when(qi == 0)
        def _ctx():
            for g in range(GQA_GROUP):
                s = _dot_nt(_head(qc_ref, g), kc_ref[...]) * scale
                o = _softmax_pv([(s, vc_ref[...])], sink_ref[hk * GQA_GROUP + g])
                _store_gated(oc_ref, zc_ref, g, o)

    @pl.when(qi >= q0)
    def _lat():
        r0 = pl.multiple_of((qi - q0) * tq, tq)
        ws = pl.multiple_of(jnp.clip(r0 - WINDOW, 0, n_lat - win_k), WINDOW)
        kw = kl_s[pl.ds(ws, win_k), :]
        vw = vl_ref[pl.ds(ws, win_k), :]
        qpos = r0 + lax.broadcasted_iota(jnp.int32, (tq, win_k), 0)
        kpos = ws + lax.broadcasted_iota(jnp.int32, (tq, win_k), 1)
        valid = jnp.abs(qpos - kpos) <= WINDOW
        cos = cos_ref[pl.ds(r0, tq), :]
        sin = sin_ref[pl.ds(r0, tq), :]
        qs = [_rope(_head(ql_ref, g).astype(F32), cos, sin).astype(BF16) for g in range(GQA_GROUP)]
        scores = [(_dot_nt(q, kw), _dot_nt(q, kc_ref[...])) for q in qs]
        for g, (s_w, s_c) in enumerate(scores):
            s_w = jnp.where(valid, s_w * scale, NEG_BIG)
            o = _softmax_pv([(s_w, vw), (s_c * scale, vc_ref[...])], sink_ref[hk * GQA_GROUP + g])
            _store_gated(ol_ref, zl_ref, g, o)


def _attn_b_kernel(*refs, with_ctx, n_lat, scale):
    if with_ctx:
        (qc_ref, zc_ref, ql_ref, zl_ref, kc_ref, vc_ref, kl_ref, vl_ref, cos_ref, sin_ref, qn_ref, kn_ref,
         oc_ref, ol_ref, kc_s, kl_s) = refs
    else:
        (ql_ref, zl_ref, kc_ref, vc_ref, kl_ref, vl_ref, cos_ref, sin_ref, qn_ref, kn_ref,
         ol_ref, kc_s, kl_s) = refs
    qi = pl.program_id(2)
    q0 = 1 if with_ctx else 0
    tq = ATTN_TQ

    @pl.when(qi == 0)
    def _prep():
        kn = kn_ref[...] * scale
        kc_s[...] = _rms(kc_ref[...].astype(F32), kn).astype(BF16)

        def body(i, carry):
            r = pl.multiple_of(i * tq, tq)
            k = _rms(kl_ref[pl.ds(r, tq), :].astype(F32), kn)
            kl_s[pl.ds(r, tq), :] = _rope(k, cos_ref[pl.ds(r, tq), :], sin_ref[pl.ds(r, tq), :]).astype(BF16)
            return carry
        lax.fori_loop(0, n_lat // tq, body, 0)

    if with_ctx:
        @pl.when(qi == 0)
        def _ctx():
            for g in range(GQA_GROUP):
                q = _rms(_head(qc_ref, g).astype(F32), qn_ref[...]).astype(BF16)
                o = _softmax_pv([(_dot_nt(q, kc_s[...]), vc_ref[...])], None)
                _store_gated(oc_ref, zc_ref, g, o)

    @pl.when(qi >= q0)
    def _lat():
        r0 = pl.multiple_of((qi - q0) * tq, tq)
        cos = cos_ref[pl.ds(r0, tq), :]
        sin = sin_ref[pl.ds(r0, tq), :]
        qs = [_rope(_rms(_head(ql_ref, g).astype(F32), qn_ref[...]), cos, sin).astype(BF16) for g in range(GQA_GROUP)]
        scores = [(_dot_nt(q, kc_s[...]), _dot_nt(q, kl_s[...])) for q in qs]
        for g, (s_c, s_l) in enumerate(scores):
            o = _softmax_pv([(s_c, vc_ref[...]), (s_l, vl_ref[...])], None)
            _store_gated(ol_ref, zl_ref, g, o)


def _attn_call(kind, p_c, p_l, extras, cos, sin, batch, with_ctx):
    m = p_c.shape[0] // batch
    n = p_l.shape[0] // batch
    tq = ATTN_TQ
    assert m == tq and n % tq == 0 and n >= tq + 2 * WINDOW
    nq = n // tq
    q0 = 1 if with_ctx else 0
    if kind == "a":
        qcol, kcol, vcol, zcol = A_Q, A_K, A_V, A_Z
    else:
        qcol, kcol, vcol, zcol = B_Q, B_K, B_V, B_Z
    gw, hd = GROUP_WIDTH, HEAD_DIM

    def lat_row(b, hk, qi):
        return b * nq + jnp.maximum(qi - q0, 0)

    def grp_c(col):
        return pl.BlockSpec((tq, gw), lambda b, hk, qi: (b, col // gw + hk))

    def grp_l(col):
        return pl.BlockSpec((tq, gw), lambda b, hk, qi: (lat_row(b, hk, qi), col // gw + hk))

    def kv(rows, col):
        return pl.BlockSpec((rows, hd), lambda b, hk, qi: (b, col // hd + hk))

    def full(a):
        return pl.BlockSpec(a.shape, lambda b, hk, qi: (0,) * a.ndim)

    in_specs, args = [], []
    if kind == "a":
        in_specs.append(pl.BlockSpec(memory_space=pltpu.SMEM))
        args.append(extras[0])
    if with_ctx:
        in_specs += [grp_c(qcol), grp_c(zcol)]
        args += [p_c, p_c]
    in_specs += [grp_l(qcol), grp_l(zcol), kv(m, kcol), kv(m, vcol), kv(n, kcol), kv(n, vcol), full(cos), full(sin)]
    args += [p_l, p_l, p_c, p_c, p_l, p_l, cos, sin]
    scratch = [pltpu.VMEM((n, hd), BF16)]
    if kind == "b":
        in_specs += [full(extras[0]), full(extras[1])]
        args += list(extras)
        scratch = [pltpu.VMEM((m, hd), BF16)] + scratch

    out_l_spec = pl.BlockSpec((tq, gw), lambda b, hk, qi: (lat_row(b, hk, qi), hk))
    out_l_shape = jax.ShapeDtypeStruct((batch * n, Q_WIDTH), BF16)
    if with_ctx:
        out_specs = [pl.BlockSpec((tq, gw), lambda b, hk, qi: (b, hk)), out_l_spec]
        out_shape = [jax.ShapeDtypeStruct((batch * m, Q_WIDTH), BF16), out_l_shape]
    else:
        out_specs, out_shape = [out_l_spec], [out_l_shape]

    body = _attn_a_kernel if kind == "a" else _attn_b_kernel
    outs = pl.pallas_call(
        functools.partial(body, with_ctx=with_ctx, n_lat=n, scale=HEAD_DIM ** -0.5),
        grid=(batch, A_KV_HEADS, nq + q0),
        in_specs=in_specs,
        out_specs=out_specs,
        out_shape=out_shape,
        scratch_shapes=scratch,
        compiler_params=_params("parallel", "parallel", "arbitrary"),
        name="attn_" + kind,
    )(*args)
    return (outs[0], outs[1]) if with_ctx else (None, outs[0])


HGRN_PAIRS = ((0, 32), (1, 2), (4, 8), (16, None))
HGRN_SELECT_LEVELS = (1, 2, 4, 8)
HGRN_SIGN_LEVELS = (1, 2, 4)
LOG2E = 1.4426950408889634


def _hgrn_constants():
    c = HGRN_CHUNK
    t = np.arange(c)[:, None]
    s = np.arange(c)[None, :]
    tri = s <= t

    def pair_mask(h):
        if h is None:
            return np.zeros((c, c), bool)
        if h == 0:
            return t == s
        blk = 2 * h
        return (t // blk == s // blk) & (t % blk >= h) & (s % blk < h)

    mw = np.stack([np.stack([np.concatenate([pair_mask(a).T if rev else pair_mask(a),
                                             pair_mask(b).T if rev else pair_mask(b)], axis=1)
                             for a, b in HGRN_PAIRS]) for rev in (False, True)])
    rows = np.broadcast_to(np.arange(c)[:, None], (c, HEAD_DIM))

    def t_role(h, rev):
        ph = rows % (2 * h)
        return (ph < h) if rev else (ph >= h)

    role = np.stack([np.stack([t_role(h, rev) for h in HGRN_SELECT_LEVELS]) for rev in (False, True)])
    sgn = np.stack([np.stack([np.where(t_role(h, rev), 1.0, -1.0) for h in HGRN_SIGN_LEVELS])
                    for rev in (False, True)])
    return (jnp.asarray(np.stack([tri, tri.T]), dtype=BF16), jnp.asarray(mw, dtype=BF16),
            jnp.asarray(role, dtype=BF16), jnp.asarray(sgn, dtype=F32))


def _hgrn_cumulate(g, tri):
    g_hi = g.astype(BF16)
    g_lo = (g - g_hi.astype(F32)).astype(BF16)
    cc = _dot(tri, jnp.concatenate([g_hi, g_lo], axis=1))
    return (cc[:, :HEAD_DIM] + cc[:, HEAD_DIM:]) * LOG2E


def _hgrn_state_operands(q, kk, cum_ref, r, rev):
    c = HGRN_CHUNK
    cum = cum_ref[pl.ds(r, c), :]
    tot = cum_ref[pl.ds(r + (0 if rev else c - 1), 1), :]
    return q * jnp.exp2(cum).astype(BF16), kk * jnp.exp2(tot - cum).astype(BF16), tot


def _hgrn_pair_operands(q, kk, cum_ref, r, role_ref, sgn_ref, rev):
    c = HGRN_CHUNK
    cum = cum_ref[pl.ds(r, c), :]

    def row(k):
        return cum_ref[pl.ds(r + k, 1), :]

    sub = lax.broadcasted_iota(jnp.int32, (8, HEAD_DIM), 0)

    def operand(h):
        blk = 2 * h
        if h >= 8:
            d_parts, o_parts = [], []
            for b in range(c // blk):
                lo, mid, hi = b * blk, b * blk + h, (b + 1) * blk
                ref = row(mid if rev else mid - 1)
                if rev:
                    d_parts += [cum[lo:mid] - ref, ref - cum[mid:hi]]
                    o_parts += [q[lo:mid], kk[mid:hi]]
                else:
                    d_parts += [ref - cum[lo:mid], cum[mid:hi] - ref]
                    o_parts += [kk[lo:mid], q[mid:hi]]
            d = jnp.concatenate(d_parts, axis=0)
            if h >= 16:
                return jnp.concatenate(o_parts, axis=0) * jnp.exp2(d).astype(BF16)
        else:
            parts = []
            for grp in range(c // 8):
                base = grp * 8
                if h == 4:
                    ref = row(base + (4 if rev else 3))
                elif h == 2:
                    ref = jnp.where(sub < 4, row(base + (2 if rev else 1)), row(base + (6 if rev else 5)))
                else:
                    k0 = base + (1 if rev else 0)
                    ref = jnp.where(sub < 2, row(k0), jnp.where(sub < 4, row(k0 + 2),
                                                               jnp.where(sub < 6, row(k0 + 4), row(k0 + 6))))
                parts.append(cum[base:base + 8] - ref)
            d = jnp.concatenate(parts, axis=0) * sgn_ref[HGRN_SIGN_LEVELS.index(h)]
        opnd = jnp.where(role_ref[HGRN_SELECT_LEVELS.index(h)] > 0, q, kk)
        return opnd * jnp.exp2(d).astype(BF16)

    def sides(item):
        if item == 0:
            return q, kk
        z = operand(item)
        return z, z

    zeros = jnp.zeros((c, HEAD_DIM), BF16)
    pairs = []
    for a, b in HGRN_PAIRS:
        la, ra = sides(a)
        if b is None:
            pairs.append((la, jnp.concatenate([ra, zeros], axis=0)))
        else:
            lb, rb = sides(b)
            pairs.append((jnp.concatenate([la, lb], axis=1),
                          jnp.concatenate([jnp.concatenate([ra, zeros], axis=1),
                                           jnp.concatenate([zeros, rb], axis=1)], axis=0)))
    return pairs


def _hgrn_blocks(x):
    return x[:HGRN_CHUNK, :HEAD_DIM], x[HGRN_CHUNK:, HEAD_DIM:]


def _hgrn_intra_products(pairs_f, pairs_b):
    out = [_hgrn_blocks(_dot_nt(jnp.concatenate([pf[0], pb[0]], axis=0), jnp.concatenate([pf[1], pb[1]], axis=0)))
           for pf, pb in zip(pairs_f, pairs_b)]
    return [o[0] for o in out], [o[1] for o in out]


def _hgrn_intra_attn(products, mw_ref, d):
    acc = None
    for j, t in enumerate(products):
        t = t.astype(BF16) * mw_ref[d, j]
        acc = t if acc is None else acc + t
    return acc


def _hgrn_outputs(fwd, bwd):
    (qf, kf, tf), af, vf, sf = fwd
    (qb, kb, tb), ab, vb, sb = bwd
    st_f, st_b = sf[...], sb[...]
    o_f, o_b = _hgrn_blocks(_dot_nt(jnp.concatenate([qf, qb], axis=0),
                                    jnp.concatenate([st_f.astype(BF16), st_b.astype(BF16)], axis=0)))
    sf[...] = st_f * jnp.exp2(tf) + _dot_tn(vf, kf)
    sb[...] = st_b * jnp.exp2(tb) + _dot_tn(vb, kb)
    i_f, i_b = _hgrn_blocks(_dot(jnp.concatenate([af, ab], axis=0),
                                 jnp.concatenate([jnp.concatenate([vf, vf], axis=0),
                                                  jnp.concatenate([vb, vb], axis=0)], axis=1)))
    return o_f + i_f, o_b + i_b


def _hgrn_kernel(*refs, with_ctx, m_ctx, n_lat):
    (qc_ref, kfc_ref, kbc_ref, ic_ref, gfc_ref, gbc_ref,
     ql_ref, kfl_ref, kbl_ref, il_ref, gfl_ref, gbl_ref) = refs[:12]
    refs = refs[12:]
    if with_ctx:
        zc_ref, refs = refs[0], refs[1:]
    zl_ref, gain_ref, tri_ref, mw_ref, role_ref, sgn_ref = refs[:6]
    refs = refs[6:]
    if with_ctx:
        oc_ref, refs = refs[0], refs[1:]
    ol_ref = refs[0]
    st_f, st_b, q_s, v_s, kkf_s, kkb_s, cumf_s, cumb_s, of_s, ob_s = refs[1:]
    chunk = HGRN_CHUNK
    nc_c, nc_l = m_ctx // chunk, n_lat // chunk

    def gather(q_ref, kf_ref, kb_ref, i_ref, gf_ref, gb_ref, n_chunks, base):
        def body(c, carry):
            r = pl.multiple_of(c * chunk, chunk)
            src = pl.ds(r, chunk)
            dst = pl.ds(pl.multiple_of(base + r, chunk), chunk)
            q_s[dst, :] = q_ref[src, :]
            v_s[dst, :] = i_ref[src, :]
            kkf_s[dst, :] = kf_ref[src, :]
            kkb_s[dst, :] = kb_ref[src, :]
            cumf_s[dst, :] = _hgrn_cumulate(gf_ref[src, :], tri_ref[0])
            cumb_s[dst, :] = _hgrn_cumulate(gb_ref[src, :], tri_ref[1])
            return carry
        lax.fori_loop(0, n_chunks, body, 0, unroll=4)

    gather(qc_ref, kfc_ref, kbc_ref, ic_ref, gfc_ref, gbc_ref, nc_c, 0)
    gather(ql_ref, kfl_ref, kbl_ref, il_ref, gfl_ref, gbl_ref, nc_l, m_ctx)

    st_f[...] = jnp.zeros_like(st_f)
    st_b[...] = jnp.zeros_like(st_b)
    n_chunks = nc_c + nc_l

    def rows(c):
        cb = jnp.where(c < nc_c, nc_c - 1 - c, n_chunks - 1 - (c - nc_c))
        return pl.multiple_of(c * chunk, chunk), pl.multiple_of(cb * chunk, chunk)

    def products(c):
        rf, rb = rows(c)
        return _hgrn_intra_products(
            _hgrn_pair_operands(q_s[pl.ds(rf, chunk), :], kkf_s[pl.ds(rf, chunk), :], cumf_s, rf,
                                role_ref.at[0], sgn_ref.at[0], False),
            _hgrn_pair_operands(q_s[pl.ds(rb, chunk), :], kkb_s[pl.ds(rb, chunk), :], cumb_s, rb,
                                role_ref.at[1], sgn_ref.at[1], True))

    def attn(prods):
        return _hgrn_intra_attn(prods[0], mw_ref, 0), _hgrn_intra_attn(prods[1], mw_ref, 1)

    def body(c, carry):
        attn_f, attn_b = carry
        nxt = products(jnp.minimum(c + 1, n_chunks - 1))
        rf, rb = rows(c)
        of_s[pl.ds(rf, chunk), :], ob_s[pl.ds(rb, chunk), :] = _hgrn_outputs(
            (_hgrn_state_operands(q_s[pl.ds(rf, chunk), :], kkf_s[pl.ds(rf, chunk), :], cumf_s, rf, False),
             attn_f, v_s[pl.ds(rf, chunk), :], st_f),
            (_hgrn_state_operands(q_s[pl.ds(rb, chunk), :], kkb_s[pl.ds(rb, chunk), :], cumb_s, rb, True),
             attn_b, v_s[pl.ds(rb, chunk), :], st_b))
        return attn(nxt)
    lax.fori_loop(0, n_chunks, body, attn(products(0)),
                  unroll=HGRN_UNROLL if n_chunks % HGRN_UNROLL == 0 else 1)

    def finish(z_ref, o_ref, rows, base):
        blk = 256

        def body(i, carry):
            r = pl.multiple_of(i * blk, blk)
            src = pl.ds(pl.multiple_of(base + r, blk), blk)
            o = _rms((of_s[src, :] + ob_s[src, :]) * (HEAD_DIM ** -0.5), gain_ref[...])
            o_ref[pl.ds(r, blk), :] = (o * _silu(z_ref[pl.ds(r, blk), :].astype(F32))).astype(BF16)
            return carry
        lax.fori_loop(0, rows // blk, body, 0)

    if with_ctx:
        finish(zc_ref, oc_ref, m_ctx, 0)
    finish(zl_ref, ol_ref, n_lat, m_ctx)


def _hgrn_call(p_c, lf_c, p_l, lf_l, gain, batch, with_ctx):
    m = p_c.shape[0] // batch
    n = p_l.shape[0] // batch
    hd = HEAD_DIM
    assert m % 256 == 0 and n % 256 == 0
    consts = _hgrn_constants()

    def col(rows, c):
        return pl.BlockSpec((rows, hd), lambda b, h: (b, c // hd + h))

    def full(a):
        return pl.BlockSpec(a.shape, lambda b, h: (0,) * a.ndim)

    def side(rows, p, lf):
        return ([col(rows, C_Q), col(rows, C_ZF), col(rows, C_ZB), col(rows, C_I), col(rows, 0), col(rows, C_HEADS * hd)],
                [p, p, p, p, lf, lf])

    specs_c, args_c = side(m, p_c, lf_c)
    specs_l, args_l = side(n, p_l, lf_l)
    in_specs, args = specs_c + specs_l, args_c + args_l
    if with_ctx:
        in_specs.append(col(m, C_Z))
        args.append(p_c)
    in_specs += [col(n, C_Z), full(gain)] + [full(a) for a in consts]
    args += [p_l, gain, *consts]

    out_l_spec = pl.BlockSpec((n, hd), lambda b, h: (b, h))
    out_l_shape = jax.ShapeDtypeStruct((batch * n, Q_WIDTH), BF16)
    if with_ctx:
        out_specs = [pl.BlockSpec((m, hd), lambda b, h: (b, h)), out_l_spec]
        out_shape = [jax.ShapeDtypeStruct((batch * m, Q_WIDTH), BF16), out_l_shape]
    else:
        out_specs, out_shape = [out_l_spec], [out_l_shape]

    outs = pl.pallas_call(
        functools.partial(_hgrn_kernel, with_ctx=with_ctx, m_ctx=m, n_lat=n),
        grid=(batch, C_HEADS),
        in_specs=in_specs,
        out_specs=out_specs,
        out_shape=out_shape,
        scratch_shapes=([pltpu.VMEM((hd, hd), F32)] * 2 + [pltpu.VMEM((m + n, hd), BF16)] * 4
                        + [pltpu.VMEM((m + n, hd), F32)] * 4),
        compiler_params=_params("parallel", "parallel"),
        name="hgrn2",
    )(*args)
    return (outs[0], outs[1]) if with_ctx else (None, outs[0])


def _merge_kernel(*refs, tail):
    (x_ref, ga_ref, gb_ref, gc_ref, ka_ref, kb_ref, kc_ref, gt_ref, wa_ref, wb_ref, wc_ref, wo_ref) = refs[:12]
    mix = jax.nn.sigmoid(ka_ref[...].astype(F32)) * _dot(ga_ref[...], wa_ref[...])
    mix = mix + jax.nn.sigmoid(kb_ref[...].astype(F32)) * _dot(gb_ref[...], wb_ref[...])
    mix = mix + jax.nn.sigmoid(kc_ref[...].astype(F32)) * _dot(gc_ref[...], wc_ref[...])
    new = x_ref[...] + gt_ref[...] * _dot(mix.astype(BF16), wo_ref[...])
    if tail == "final":
        refs[13][...] = _rms(new, refs[12][...])
    else:
        g_ref, sc_ref, sh_ref, o_ref, h_ref = refs[12:]
        o_ref[...] = new
        h_ref[...] = _modulated_norm(new, g_ref[...], sc_ref[...], sh_ref[...]).astype(BF16)


def _merge_call(x, ga, gb, gc, p, mod, wa, wb, wc, wo, rows_per_mod, mod_row0, final_g=None, next_norm=None):
    rows, d = x.shape
    tm = _row_tile(rows, rows_per_mod, 256)

    def rowblk(width, colblk=0):
        return pl.BlockSpec((tm, width), lambda i: (i, colblk))

    def weight(w):
        return pl.BlockSpec(w.shape, lambda i: (0, 0), pipeline_mode=pl.Buffered(1))

    def mod_spec(which):
        return pl.BlockSpec((None, None, 1, d), lambda i: (mod_row0 + (i * tm) // rows_per_mod, which, 0, 0))

    vec = pl.BlockSpec((1, d), lambda i: (0, 0))
    in_specs = [rowblk(d), rowblk(Q_WIDTH), rowblk(Q_WIDTH), rowblk(Q_WIDTH),
                rowblk(d, GATES // d), rowblk(d, GATES // d + 1), rowblk(d, GATES // d + 2),
                mod_spec(2), weight(wa), weight(wb), weight(wc), weight(wo)]
    args = [x, ga, gb, gc, p, p, p, mod, wa, wb, wc, wo]
    if final_g is not None:
        in_specs.append(vec)
        args.append(final_g.reshape(1, d))
        out_specs, out_shape = rowblk(d), jax.ShapeDtypeStruct((rows, d), F32)
    else:
        next_g, next_mod = next_norm
        in_specs += [vec, mod_spec(1), mod_spec(0)]
        args += [next_g.reshape(1, d), next_mod, next_mod]
        out_specs = [rowblk(d), rowblk(d)]
        out_shape = [jax.ShapeDtypeStruct((rows, d), F32), jax.ShapeDtypeStruct((rows, d), BF16)]
    return pl.pallas_call(
        functools.partial(_merge_kernel, tail="final" if final_g is not None else "next"),
        grid=(rows // tm,),
        in_specs=in_specs,
        out_specs=out_specs,
        out_shape=out_shape,
        compiler_params=_params("parallel"),
        name="merge_out",
    )(*args)


def _rope_tables(n):
    rows = n // GRID_W
    row = jnp.repeat(jnp.arange(rows, dtype=F32), GRID_W)
    col = jnp.tile(jnp.arange(GRID_W, dtype=F32), rows)
    n_freq = HEAD_DIM // 4
    inv_freq = ROPE_THETA ** (-jnp.arange(n_freq, dtype=F32) / n_freq)
    ang = jnp.concatenate([row[:, None] * inv_freq, col[:, None] * inv_freq], axis=-1)
    ang = jnp.concatenate([ang, ang], axis=-1)
    sign = jnp.where(jnp.arange(HEAD_DIM) < HEAD_DIM // 2, -1.0, 1.0).astype(F32)
    return jnp.cos(ang), jnp.sin(ang) * sign


def kernel(x, c, ctx, c_ctx, ada_w, ada_b, norm_g, w_in, a_sink, b_q_norm, b_k_norm, c_lower_bound, c_out_norm,
           w_branch_a, w_branch_b, w_branch_c, w_out, final_norm_g):
    batch, n, d = x.shape
    m = ctx.shape[1]
    depth = ada_w.shape[0]
    assert batch + 1 <= MOD_ROWS
    cos, sin = _rope_tables(n)
    lb_all = jnp.cumsum(jax.nn.softmax(c_lower_bound.astype(F32), axis=0), axis=0)
    lb_all = lb_all - lb_all[0:1]

    cv = jnp.zeros((MOD_ROWS, d), F32).at[:batch].set(c).at[batch].set(c_ctx)
    mod = _ada_call(cv, ada_w, ada_b).reshape(depth, MOD_ROWS, 3, 1, d)

    x_l = x.reshape(batch * n, d)
    x_c = ctx.reshape(batch * m, d)
    h_c = h_l = None
    for l in range(depth):
        with_ctx = l < depth - 1
        norm = (norm_g[l], mod[l]) if l == 0 else None
        p_c, lf_c = _inproj_call(x_c if l == 0 else h_c, norm, w_in, l, lb_all[l], batch * m, batch,
                                 col_tiles=None if with_ctx else LAST_LAYER_CTX_TILES)
        p_l, lf_l = _inproj_call(x_l if l == 0 else h_l, norm, w_in, l, lb_all[l], n, 0)
        ga_c, ga_l = _attn_call("a", p_c, p_l, (a_sink[l],), cos, sin, batch, with_ctx)
        gb_c, gb_l = _attn_call("b", p_c, p_l, (b_q_norm[l].reshape(1, -1), b_k_norm[l].reshape(1, -1)),
                                cos, sin, batch, with_ctx)
        gc_c, gc_l = _hgrn_call(p_c, lf_c, p_l, lf_l, c_out_norm[l].reshape(1, -1), batch, with_ctx)
        ws = (w_branch_a[l].astype(BF16), w_branch_b[l].astype(BF16), w_branch_c[l].astype(BF16),
              w_out[l].astype(BF16))
        if with_ctx:
            nxt = (norm_g[l + 1], mod[l + 1])
            x_c, h_c = _merge_call(x_c, ga_c, gb_c, gc_c, p_c, mod[l], *ws, batch * m, batch, next_norm=nxt)
            x_l, h_l = _merge_call(x_l, ga_l, gb_l, gc_l, p_l, mod[l], *ws, n, 0, next_norm=nxt)
        else:
            x_l = _merge_call(x_l, ga_l, gb_l, gc_l, p_l, mod[l], *ws, n, 0, final_g=final_norm_g)
    return x_l.reshape(batch, n, d)
```

```python
import functools

import numpy as np
import jax
import jax.numpy as jnp
from jax import lax
from jax.experimental import pallas as pl
from jax.experimental.pallas import tpu as pltpu

F32 = jnp.float32
BF16 = jnp.bfloat16

HEAD_DIM = 128
A_HEADS, A_KV_HEADS = 8, 2
B_HEADS, B_KV_HEADS = 8, 2
C_HEADS = 8
WINDOW = 128
GRID_W = 64
ROPE_THETA = 10000.0
NORM_EPS = 1e-6
N_BRANCH = 3
GQA_GROUP = A_HEADS // A_KV_HEADS
Q_WIDTH = A_HEADS * HEAD_DIM
KV_WIDTH = A_KV_HEADS * HEAD_DIM
GROUP_WIDTH = GQA_GROUP * HEAD_DIM

A_Q, A_K, A_V, A_Z = 0, 1024, 1280, 1536
B_Q, B_K, B_V, B_Z = 2560, 3584, 3840, 4096
C_Q, C_ZF, C_ZB, C_I, C_Z = 5120, 6144, 7168, 8192, 9216
GATES = 10240
IN_PROJ_TN = 1024
GATE_SLAB = 256
LAST_LAYER_CTX_TILES = tuple(sorted({c // IN_PROJ_TN for c in (A_K, A_V, B_K, B_V, C_Q, C_ZF, C_ZB, C_I)}))

ATTN_TQ = 256
HGRN_CHUNK = 64
HGRN_UNROLL = 4
MOD_ROWS = 16
LOG2E = 1.4426950408889634
NEG_BIG = -1e30
VMEM_LIMIT = 56 * 1024 * 1024


def _dot(a, b):
    return jnp.dot(a, b, preferred_element_type=F32)


def _dot_nt(a, b):
    return lax.dot_general(a, b, (((1,), (1,)), ((), ())), preferred_element_type=F32)


def _dot_tn(a, b):
    return lax.dot_general(a, b, (((0,), (0,)), ((), ())), preferred_element_type=F32)


def _rms(x, g):
    return x * lax.rsqrt(jnp.mean(x * x, axis=-1, keepdims=True) + NORM_EPS) * g


def _rope(x, cos, sin_signed):
    return x * cos + pltpu.roll(x, HEAD_DIM // 2, 1) * sin_signed


def _silu(x):
    return x * jax.nn.sigmoid(x)


def _params(*sem, flags=None):
    return pltpu.CompilerParams(dimension_semantics=sem, vmem_limit_bytes=VMEM_LIMIT, flags=flags)


def _ada_kernel(cv_ref, w_ref, b_ref, o_ref):
    s = _silu(cv_ref[...]).astype(BF16)
    o_ref[...] = _dot(s, w_ref[...].astype(BF16)) + b_ref[...]


def _ada_call(cv, ada_w, ada_b, tn=512):
    depth, d, n3 = ada_w.shape
    return pl.pallas_call(
        _ada_kernel,
        grid=(depth, n3 // tn),
        in_specs=[
            pl.BlockSpec((MOD_ROWS, d), lambda l, j: (0, 0)),
            pl.BlockSpec((None, d, tn), lambda l, j: (l, 0, j)),
            pl.BlockSpec((None, 1, tn), lambda l, j: (l, 0, j)),
        ],
        out_specs=pl.BlockSpec((None, MOD_ROWS, tn), lambda l, j: (l, 0, j)),
        out_shape=jax.ShapeDtypeStruct((depth, MOD_ROWS, n3), F32),
        compiler_params=_params("parallel", "parallel"),
        name="ada_mod",
    )(cv, ada_w, ada_b.reshape(depth, 1, n3))


def _modulated_norm(x, g, sc, sh):
    return _rms(x, g) * (1.0 + sc) + sh


def _forget_gate(z, lb):
    log_lb, log1m_lb, one_m_lb = jnp.log(lb), jnp.log(1.0 - lb), 1.0 - lb
    u = jnp.exp(-jnp.abs(z))
    log_sig = jnp.minimum(z, 0.0) - jnp.log(1.0 + u)
    b = log1m_lb + log_sig
    g = jnp.maximum(log_lb, b) + jnp.log(1.0 + jnp.exp(-jnp.abs(log_lb - b)))
    kk = one_m_lb * jnp.where(z >= 0.0, u, 1.0) * (1.0 / (1.0 + u))
    return g, kk


def _inproj_kernel(*refs, pre_normed, tiles):
    if pre_normed:
        h_ref, w_ref, lb_ref, o_ref, lf_ref = refs
    else:
        x_ref, g_ref, sc_ref, sh_ref, w_ref, lb_ref, o_ref, lf_ref, h_ref = refs

        @pl.when(pl.program_id(1) == 0)
        def _():
            h_ref[...] = _modulated_norm(x_ref[...], g_ref[...], sc_ref[...], sh_ref[...]).astype(BF16)

    tile = pl.program_id(1) if tiles is None else _lookup(tiles, pl.program_id(1))
    is_gate = (tile == C_ZF // IN_PROJ_TN) | (tile == C_ZB // IN_PROJ_TN)

    @pl.when(jnp.logical_not(is_gate))
    def _():
        o_ref[...] = _dot(h_ref[...], w_ref[...].astype(BF16)).astype(BF16)

    @pl.when(is_gate)
    def _():
        acc = _dot(h_ref[...], w_ref[...].astype(BF16))
        for c0 in range(0, IN_PROJ_TN, GATE_SLAB):
            g, kk = _forget_gate(acc[:, c0:c0 + GATE_SLAB], lb_ref[:, c0:c0 + GATE_SLAB])
            o_ref[:, c0:c0 + GATE_SLAB] = kk.astype(BF16)
            lf_ref[:, c0:c0 + GATE_SLAB] = g


def _lookup(table, j):
    v = table[-1]
    for k in range(len(table) - 2, -1, -1):
        v = jnp.where(j == k, table[k], v)
    return v


def _row_tile(rows, rows_per_mod, cap):
    t = cap
    while rows % t or rows_per_mod % t:
        t //= 2
    return t


def _inproj_call(x, norm, w_all, layer, lb, rows_per_mod, mod_row0, col_tiles=None):
    rows, d = x.shape
    n = w_all.shape[2]
    tn = IN_PROJ_TN
    tm = _row_tile(rows, rows_per_mod, 1024)
    tiles = tuple(range(n // tn)) if col_tiles is None else tuple(col_tiles)
    zb_tile = C_ZB // tn

    def col(j):
        return j if col_tiles is None else _lookup(tiles, j)

    def dirn(j):
        return jnp.where(col(j) >= zb_tile, 1, 0)

    in_specs = [pl.BlockSpec((tm, d), lambda i, j: (i, 0))]
    args = [x]
    scratch = []
    if norm is not None:
        g, mod = norm

        def mod_spec(which):
            return pl.BlockSpec((None, None, 1, d),
                                lambda i, j: (mod_row0 + (i * tm) // rows_per_mod, which, 0, 0))

        in_specs += [pl.BlockSpec((1, d), lambda i, j: (0, 0)), mod_spec(1), mod_spec(0)]
        args += [g.reshape(1, d), mod, mod]
        scratch = scratch + [pltpu.VMEM((tm, d), BF16)]
    in_specs += [pl.BlockSpec((None, d, tn), lambda i, j: (layer, 0, col(j))),
                 pl.BlockSpec((None, 1, tn), lambda i, j: (dirn(j), 0, 0))]
    args += [w_all, lb.reshape(2, 1, tn)]
    return pl.pallas_call(
        functools.partial(_inproj_kernel, pre_normed=norm is None, tiles=None if col_tiles is None else tiles),
        grid=(rows // tm, len(tiles)),
        in_specs=in_specs,
        out_specs=[pl.BlockSpec((tm, tn), lambda i, j: (i, col(j))),
                   pl.BlockSpec((tm, tn), lambda i, j: (i, dirn(j)))],
        out_shape=[jax.ShapeDtypeStruct((rows, n), BF16), jax.ShapeDtypeStruct((rows, 2 * tn), F32)],
        scratch_shapes=scratch,
        compiler_params=_params("parallel", "arbitrary"),
        name="in_proj",
    )(*args)


def _softmax_pv(parts, extra_logit):
    m = functools.reduce(jnp.maximum, [jnp.max(s, axis=-1, keepdims=True) for s, _ in parts])
    if extra_logit is not None:
        m = jnp.maximum(m, extra_logit)
    ps = [jnp.exp2(s - m) for s, _ in parts]
    l = functools.reduce(jnp.add, [jnp.sum(p, axis=-1, keepdims=True) for p in ps])
    if extra_logit is not None:
        l = l + jnp.exp2(extra_logit - m)
    o = functools.reduce(jnp.add, [_dot(p.astype(BF16), v) for p, (_, v) in zip(ps, parts)])
    return o * (1.0 / l)


def _head(ref, g):
    return ref[:, g * HEAD_DIM:(g + 1) * HEAD_DIM]


def _store_gated(o_ref, z_ref, g, o):
    z = _head(z_ref, g).astype(F32)
    o_ref[:, g * HEAD_DIM:(g + 1) * HEAD_DIM] = (o * _silu(z)).astype(BF16)


def _attn_a_kernel(*refs, with_ctx, n_lat, scale):
    if with_ctx:
        (sink_ref, qc_ref, zc_ref, ql_ref, zl_ref, kc_ref, vc_ref, kl_ref, vl_ref, cos_ref, sin_ref,
         oc_ref, ol_ref, kl_s) = refs
    else:
        (sink_ref, ql_ref, zl_ref, kc_ref, vc_ref, kl_ref, vl_ref, cos_ref, sin_ref, ol_ref, kl_s) = refs
    hk = pl.program_id(1)
    qi = pl.program_id(2)
    q0 = 1 if with_ctx else 0
    tq = ATTN_TQ
    win_k = tq + 2 * WINDOW

    @pl.when(qi == 0)
    def _prep():
        def body(i, carry):
            r = pl.multiple_of(i * tq, tq)
            k = kl_ref[pl.ds(r, tq), :].astype(F32)
            kl_s[pl.ds(r, tq), :] = _rope(k, cos_ref[pl.ds(r, tq), :], sin_ref[pl.ds(r, tq), :]).astype(BF16)
            return carry
        lax.fori_loop(0, n_lat // tq, body, 0)

    if with_ctx:
        @pl.when(qi == 0)
        def _ctx():
            for g in range(GQA_GROUP):
                s = _dot_nt(_head(qc_ref, g), kc_ref[...]) * scale
                o = _softmax_pv([(s, vc_ref[...])], sink_ref[hk * GQA_GROUP + g] * LOG2E)
                _store_gated(oc_ref, zc_ref, g, o)

    @pl.when(qi >= q0)
    def _lat():
        r0 = pl.multiple_of((qi - q0) * tq, tq)
        ws = pl.multiple_of(jnp.clip(r0 - WINDOW, 0, n_lat - win_k), WINDOW)
        kw = kl_s[pl.ds(ws, win_k), :]
        vw = vl_ref[pl.ds(ws, win_k), :]
        qpos = r0 + lax.broadcasted_iota(jnp.int32, (tq, win_k), 0)
        kpos = ws + lax.broadcasted_iota(jnp.int32, (tq, win_k), 1)
        valid = jnp.abs(qpos - kpos) <= WINDOW
        cos = cos_ref[pl.ds(r0, tq), :]
        sin = sin_ref[pl.ds(r0, tq), :]
        qs = [_rope(_head(ql_ref, g).astype(F32), cos, sin).astype(BF16) for g in range(GQA_GROUP)]
        scores = [(_dot_nt(q, kw), _dot_nt(q, kc_ref[...])) for q in qs]
        for g, (s_w, s_c) in enumerate(scores):
            s_w = jnp.where(valid, s_w * scale, NEG_BIG)
            o = _softmax_pv([(s_w, vw), (s_c * scale, vc_ref[...])], sink_ref[hk * GQA_GROUP + g] * LOG2E)
            _store_gated(ol_ref, zl_ref, g, o)


def _attn_b_kernel(*refs, with_ctx, n_lat, scale):
    if with_ctx:
        (qc_ref, zc_ref, ql_ref, zl_ref, kc_ref, vc_ref, kl_ref, vl_ref, cos_ref, sin_ref, qn_ref, kn_ref,
         oc_ref, ol_ref, kc_s, kl_s) = refs
    else:
        (ql_ref, zl_ref, kc_ref, vc_ref, kl_ref, vl_ref, cos_ref, sin_ref, qn_ref, kn_ref,
         ol_ref, kc_s, kl_s) = refs
    qi = pl.program_id(2)
    q0 = 1 if with_ctx else 0
    tq = ATTN_TQ

    @pl.when(qi == 0)
    def _prep():
        kn = kn_ref[...] * scale
        kc_s[...] = _rms(kc_ref[...].astype(F32), kn).astype(BF16)

        def body(i, carry):
            r = pl.multiple_of(i * tq, tq)
            k = _rms(kl_ref[pl.ds(r, tq), :].astype(F32), kn)
            kl_s[pl.ds(r, tq), :] = _rope(k, cos_ref[pl.ds(r, tq), :], sin_ref[pl.ds(r, tq), :]).astype(BF16)
            return carry
        lax.fori_loop(0, n_lat // tq, body, 0)

    if with_ctx:
        @pl.when(qi == 0)
        def _ctx():
            for g in range(GQA_GROUP):
                q = _rms(_head(qc_ref, g).astype(F32), qn_ref[...]).astype(BF16)
                o = _softmax_pv([(_dot_nt(q, kc_s[...]), vc_ref[...])], None)
                _store_gated(oc_ref, zc_ref, g, o)

    @pl.when(qi >= q0)
    def _lat():
        r0 = pl.multiple_of((qi - q0) * tq, tq)
        cos = cos_ref[pl.ds(r0, tq), :]
        sin = sin_ref[pl.ds(r0, tq), :]
        qs = [_rope(_rms(_head(ql_ref, g).astype(F32), qn_ref[...]), cos, sin).astype(BF16) for g in range(GQA_GROUP)]
        scores = [(_dot_nt(q, kc_s[...]), _dot_nt(q, kl_s[...])) for q in qs]
        for g, (s_c, s_l) in enumerate(scores):
            o = _softmax_pv([(s_c, vc_ref[...]), (s_l, vl_ref[...])], None)
            _store_gated(ol_ref, zl_ref, g, o)


def _attn_call(kind, p_c, p_l, extras, cos, sin, batch, with_ctx):
    m = p_c.shape[0] // batch
    n = p_l.shape[0] // batch
    tq = ATTN_TQ
    assert m == tq and n % tq == 0 and n >= tq + 2 * WINDOW
    nq = n // tq
    q0 = 1 if with_ctx else 0
    if kind == "a":
        qcol, kcol, vcol, zcol = A_Q, A_K, A_V, A_Z
    else:
        qcol, kcol, vcol, zcol = B_Q, B_K, B_V, B_Z
    gw, hd = GROUP_WIDTH, HEAD_DIM

    def lat_row(b, hk, qi):
        return b * nq + jnp.maximum(qi - q0, 0)

    def grp_c(col):
        return pl.BlockSpec((tq, gw), lambda b, hk, qi: (b, col // gw + hk))

    def grp_l(col):
        return pl.BlockSpec((tq, gw), lambda b, hk, qi: (lat_row(b, hk, qi), col // gw + hk))

    def kv(rows, col):
        return pl.BlockSpec((rows, hd), lambda b, hk, qi: (b, col // hd + hk))

    def full(a):
        return pl.BlockSpec(a.shape, lambda b, hk, qi: (0,) * a.ndim)

    in_specs, args = [], []
    if kind == "a":
        in_specs.append(pl.BlockSpec(memory_space=pltpu.SMEM))
        args.append(extras[0])
    if with_ctx:
        in_specs += [grp_c(qcol), grp_c(zcol)]
        args += [p_c, p_c]
    in_specs += [grp_l(qcol), grp_l(zcol), kv(m, kcol), kv(m, vcol), kv(n, kcol), kv(n, vcol), full(cos), full(sin)]
    args += [p_l, p_l, p_c, p_c, p_l, p_l, cos, sin]
    scratch = [pltpu.VMEM((n, hd), BF16)]
    if kind == "b":
        in_specs += [full(extras[0]), full(extras[1])]
        args += list(extras)
        scratch = [pltpu.VMEM((m, hd), BF16)] + scratch

    out_l_spec = pl.BlockSpec((tq, gw), lambda b, hk, qi: (lat_row(b, hk, qi), hk))
    out_l_shape = jax.ShapeDtypeStruct((batch * n, Q_WIDTH), BF16)
    if with_ctx:
        out_specs = [pl.BlockSpec((tq, gw), lambda b, hk, qi: (b, hk)), out_l_spec]
        out_shape = [jax.ShapeDtypeStruct((batch * m, Q_WIDTH), BF16), out_l_shape]
    else:
        out_specs, out_shape = [out_l_spec], [out_l_shape]

    body = _attn_a_kernel if kind == "a" else _attn_b_kernel
    outs = pl.pallas_call(
        functools.partial(body, with_ctx=with_ctx, n_lat=n, scale=HEAD_DIM ** -0.5 * LOG2E),
        grid=(batch, A_KV_HEADS, nq + q0),
        in_specs=in_specs,
        out_specs=out_specs,
        out_shape=out_shape,
        scratch_shapes=scratch,
        compiler_params=_params("parallel", "parallel", "arbitrary"),
        name="attn_" + kind,
    )(*args)
    return (outs[0], outs[1]) if with_ctx else (None, outs[0])


HGRN_PAIRS = ((0, 32), (1, 2), (4, 8), (16, None))
HGRN_SELECT_LEVELS = (1, 2, 4, 8)
HGRN_SIGN_LEVELS = (1, 2, 4)


def _hgrn_constants():
    c = HGRN_CHUNK
    t = np.arange(c)[:, None]
    s = np.arange(c)[None, :]
    tri = s <= t

    def pair_mask(h):
        if h is None:
            return np.zeros((c, c), bool)
        if h == 0:
            return t == s
        blk = 2 * h
        return (t // blk == s // blk) & (t % blk >= h) & (s % blk < h)

    mw = np.stack([np.stack([np.concatenate([pair_mask(a).T if rev else pair_mask(a),
                                             pair_mask(b).T if rev else pair_mask(b)], axis=1)
                             for a, b in HGRN_PAIRS]) for rev in (False, True)])
    rows = np.broadcast_to(np.arange(c)[:, None], (c, HEAD_DIM))

    def t_role(h, rev):
        ph = rows % (2 * h)
        return (ph < h) if rev else (ph >= h)

    role = np.stack([np.stack([t_role(h, rev) for h in HGRN_SELECT_LEVELS]) for rev in (False, True)])
    sgn = np.stack([np.stack([np.where(t_role(h, rev), 1.0, -1.0) for h in HGRN_SIGN_LEVELS])
                    for rev in (False, True)])
    return (jnp.asarray(np.stack([tri, tri.T]), dtype=BF16), jnp.asarray(mw, dtype=BF16),
            jnp.asarray(role, dtype=BF16), jnp.asarray(sgn, dtype=F32))


def _hgrn_cumulate(g, tri):
    g_hi = g.astype(BF16)
    g_lo = (g - g_hi.astype(F32)).astype(BF16)
    cc = _dot(tri, jnp.concatenate([g_hi, g_lo], axis=1))
    return (cc[:, :HEAD_DIM] + cc[:, HEAD_DIM:]) * LOG2E


def _hgrn_state_operands(q, kk, cum_ref, r, rev):
    c = HGRN_CHUNK
    cum = cum_ref[pl.ds(r, c), :]
    tot = cum_ref[pl.ds(r + (0 if rev else c - 1), 1), :]
    return q * jnp.exp2(cum).astype(BF16), kk * jnp.exp2(tot - cum).astype(BF16), tot


def _hgrn_pair_operands(q, kk, cum_ref, r, role_ref, sgn_ref, rev):
    c = HGRN_CHUNK
    cum = cum_ref[pl.ds(r, c), :]

    def row(k):
        return cum_ref[pl.ds(r + k, 1), :]

    sub = lax.broadcasted_iota(jnp.int32, (8, HEAD_DIM), 0)

    def operand(h):
        blk = 2 * h
        if h >= 8:
            d_parts, o_parts = [], []
            for b in range(c // blk):
                lo, mid, hi = b * blk, b * blk + h, (b + 1) * blk
                ref = row(mid if rev else mid - 1)
                if rev:
                    d_parts += [cum[lo:mid] - ref, ref - cum[mid:hi]]
                    o_parts += [q[lo:mid], kk[mid:hi]]
                else:
                    d_parts += [ref - cum[lo:mid], cum[mid:hi] - ref]
                    o_parts += [kk[lo:mid], q[mid:hi]]
            d = jnp.concatenate(d_parts, axis=0)
            if h >= 16:
                return jnp.concatenate(o_parts, axis=0) * jnp.exp2(d).astype(BF16)
        else:
            parts = []
            for grp in range(c // 8):
                base = grp * 8
                if h == 4:
                    ref = row(base + (4 if rev else 3))
                elif h == 2:
                    ref = jnp.where(sub < 4, row(base + (2 if rev else 1)), row(base + (6 if rev else 5)))
                else:
                    k0 = base + (1 if rev else 0)
                    ref = jnp.where(sub < 2, row(k0), jnp.where(sub < 4, row(k0 + 2),
                                                               jnp.where(sub < 6, row(k0 + 4), row(k0 + 6))))
                parts.append(cum[base:base + 8] - ref)
            d = jnp.concatenate(parts, axis=0) * sgn_ref[HGRN_SIGN_LEVELS.index(h)]
        opnd = jnp.where(role_ref[HGRN_SELECT_LEVELS.index(h)] > 0, q, kk)
        return opnd * jnp.exp2(d).astype(BF16)

    def sides(item):
        if item == 0:
            return q, kk
        z = operand(item)
        return z, z

    zeros = jnp.zeros((c, HEAD_DIM), BF16)
    pairs = []
    for a, b in HGRN_PAIRS:
        la, ra = sides(a)
        if b is None:
            pairs.append((la, jnp.concatenate([ra, zeros], axis=0)))
        else:
            lb, rb = sides(b)
            pairs.append((jnp.concatenate([la, lb], axis=1),
                          jnp.concatenate([jnp.concatenate([ra, zeros], axis=1),
                                           jnp.concatenate([zeros, rb], axis=1)], axis=0)))
    return pairs


def _hgrn_blocks(x):
    return x[:HGRN_CHUNK, :HEAD_DIM], x[HGRN_CHUNK:, HEAD_DIM:]


def _hgrn_intra_products(pairs_f, pairs_b):
    out = [_hgrn_blocks(_dot_nt(jnp.concatenate([pf[0], pb[0]], axis=0), jnp.concatenate([pf[1], pb[1]], axis=0)))
           for pf, pb in zip(pairs_f, pairs_b)]
    return [o[0] for o in out], [o[1] for o in out]


def _hgrn_intra_attn(products, mw_ref, d):
    acc = None
    for j, t in enumerate(products):
        t = t.astype(BF16) * mw_ref[d, j]
        acc = t if acc is None else acc + t
    return acc


def _hgrn_outputs(fwd, bwd):
    (qf, kf, tf), af, vf, sf = fwd
    (qb, kb, tb), ab, vb, sb = bwd
    st_f, st_b = sf[...], sb[...]
    o_f, o_b = _hgrn_blocks(_dot_nt(jnp.concatenate([qf, qb], axis=0),
                                    jnp.concatenate([st_f.astype(BF16), st_b.astype(BF16)], axis=0)))
    sf[...] = st_f * jnp.exp2(tf) + _dot_tn(vf, kf)
    sb[...] = st_b * jnp.exp2(tb) + _dot_tn(vb, kb)
    i_f, i_b = _hgrn_blocks(_dot(jnp.concatenate([af, ab], axis=0),
                                 jnp.concatenate([jnp.concatenate([vf, vf], axis=0),
                                                  jnp.concatenate([vb, vb], axis=0)], axis=1)))
    return o_f + i_f, o_b + i_b


def _hgrn_kernel(*refs, with_ctx, m_ctx, n_lat):
    (qc_ref, kfc_ref, kbc_ref, ic_ref, gfc_ref, gbc_ref,
     ql_ref, kfl_ref, kbl_ref, il_ref, gfl_ref, gbl_ref) = refs[:12]
    refs = refs[12:]
    if with_ctx:
        zc_ref, refs = refs[0], refs[1:]
    zl_ref, gain_ref, tri_ref, mw_ref, role_ref, sgn_ref = refs[:6]
    refs = refs[6:]
    if with_ctx:
        oc_ref, refs = refs[0], refs[1:]
    ol_ref = refs[0]
    st_f, st_b, q_s, v_s, kkf_s, kkb_s, cumf_s, cumb_s, of_s, ob_s = refs[1:]
    chunk = HGRN_CHUNK
    nc_c, nc_l = m_ctx // chunk, n_lat // chunk

    def gather(q_ref, kf_ref, kb_ref, i_ref, gf_ref, gb_ref, n_chunks, base):
        def body(c, carry):
            r = pl.multiple_of(c * chunk, chunk)
            src = pl.ds(r, chunk)
            dst = pl.ds(pl.multiple_of(base + r, chunk), chunk)
            q_s[dst, :] = q_ref[src, :]
            v_s[dst, :] = i_ref[src, :]
            kkf_s[dst, :] = kf_ref[src, :]
            kkb_s[dst, :] = kb_ref[src, :]
            cumf_s[dst, :] = _hgrn_cumulate(gf_ref[src, :], tri_ref[0])
            cumb_s[dst, :] = _hgrn_cumulate(gb_ref[src, :], tri_ref[1])
            return carry
        lax.fori_loop(0, n_chunks, body, 0, unroll=4)

    gather(qc_ref, kfc_ref, kbc_ref, ic_ref, gfc_ref, gbc_ref, nc_c, 0)
    gather(ql_ref, kfl_ref, kbl_ref, il_ref, gfl_ref, gbl_ref, nc_l, m_ctx)

    st_f[...] = jnp.zeros_like(st_f)
    st_b[...] = jnp.zeros_like(st_b)
    n_chunks = nc_c + nc_l

    def rows(c):
        cb = jnp.where(c < nc_c, nc_c - 1 - c, n_chunks - 1 - (c - nc_c))
        return pl.multiple_of(c * chunk, chunk), pl.multiple_of(cb * chunk, chunk)

    def products(c):
        rf, rb = rows(c)
        return _hgrn_intra_products(
            _hgrn_pair_operands(q_s[pl.ds(rf, chunk), :], kkf_s[pl.ds(rf, chunk), :], cumf_s, rf,
                                role_ref.at[0], sgn_ref.at[0], False),
            _hgrn_pair_operands(q_s[pl.ds(rb, chunk), :], kkb_s[pl.ds(rb, chunk), :], cumb_s, rb,
                                role_ref.at[1], sgn_ref.at[1], True))

    def attn(prods):
        return _hgrn_intra_attn(prods[0], mw_ref, 0), _hgrn_intra_attn(prods[1], mw_ref, 1)

    def body(c, carry):
        attn_f, attn_b = carry
        nxt = products(jnp.minimum(c + 1, n_chunks - 1))
        rf, rb = rows(c)
        of_s[pl.ds(rf, chunk), :], ob_s[pl.ds(rb, chunk), :] = _hgrn_outputs(
            (_hgrn_state_operands(q_s[pl.ds(rf, chunk), :], kkf_s[pl.ds(rf, chunk), :], cumf_s, rf, False),
             attn_f, v_s[pl.ds(rf, chunk), :], st_f),
            (_hgrn_state_operands(q_s[pl.ds(rb, chunk), :], kkb_s[pl.ds(rb, chunk), :], cumb_s, rb, True),
             attn_b, v_s[pl.ds(rb, chunk), :], st_b))
        return attn(nxt)
    lax.fori_loop(0, n_chunks, body, attn(products(0)),
                  unroll=HGRN_UNROLL if n_chunks % HGRN_UNROLL == 0 else 1)

    def finish(z_ref, o_ref, rows, base):
        blk = 256

        def body(i, carry):
            r = pl.multiple_of(i * blk, blk)
            src = pl.ds(pl.multiple_of(base + r, blk), blk)
            o = _rms((of_s[src, :] + ob_s[src, :]) * (HEAD_DIM ** -0.5), gain_ref[...])
            o_ref[pl.ds(r, blk), :] = (o * _silu(z_ref[pl.ds(r, blk), :].astype(F32))).astype(BF16)
            return carry
        lax.fori_loop(0, rows // blk, body, 0)

    if with_ctx:
        finish(zc_ref, oc_ref, m_ctx, 0)
    finish(zl_ref, ol_ref, n_lat, m_ctx)


def _hgrn_call(p_c, lf_c, p_l, lf_l, gain, batch, with_ctx):
    m = p_c.shape[0] // batch
    n = p_l.shape[0] // batch
    hd = HEAD_DIM
    assert m % 256 == 0 and n % 256 == 0
    consts = _hgrn_constants()

    def col(rows, c):
        return pl.BlockSpec((rows, hd), lambda b, h: (b, c // hd + h))

    def full(a):
        return pl.BlockSpec(a.shape, lambda b, h: (0,) * a.ndim)

    def side(rows, p, lf):
        return ([col(rows, C_Q), col(rows, C_ZF), col(rows, C_ZB), col(rows, C_I), col(rows, 0), col(rows, C_HEADS * hd)],
                [p, p, p, p, lf, lf])

    specs_c, args_c = side(m, p_c, lf_c)
    specs_l, args_l = side(n, p_l, lf_l)
    in_specs, args = specs_c + specs_l, args_c + args_l
    if with_ctx:
        in_specs.append(col(m, C_Z))
        args.append(p_c)
    in_specs += [col(n, C_Z), full(gain)] + [full(a) for a in consts]
    args += [p_l, gain, *consts]

    out_l_spec = pl.BlockSpec((n, hd), lambda b, h: (b, h))
    out_l_shape = jax.ShapeDtypeStruct((batch * n, Q_WIDTH), BF16)
    if with_ctx:
        out_specs = [pl.BlockSpec((m, hd), lambda b, h: (b, h)), out_l_spec]
        out_shape = [jax.ShapeDtypeStruct((batch * m, Q_WIDTH), BF16), out_l_shape]
    else:
        out_specs, out_shape = [out_l_spec], [out_l_shape]

    outs = pl.pallas_call(
        functools.partial(_hgrn_kernel, with_ctx=with_ctx, m_ctx=m, n_lat=n),
        grid=(batch, C_HEADS),
        in_specs=in_specs,
        out_specs=out_specs,
        out_shape=out_shape,
        scratch_shapes=([pltpu.VMEM((hd, hd), F32)] * 2 + [pltpu.VMEM((m + n, hd), BF16)] * 4
                        + [pltpu.VMEM((m + n, hd), F32)] * 4),
        compiler_params=_params("parallel", "parallel"),
        name="hgrn2",
    )(*args)
    return (outs[0], outs[1]) if with_ctx else (None, outs[0])


def _merge_kernel(*refs, tail):
    (x_ref, ga_ref, gb_ref, gc_ref, ka_ref, kb_ref, kc_ref, gt_ref, wa_ref, wb_ref, wc_ref, wo_ref) = refs[:12]
    mix = jax.nn.sigmoid(ka_ref[...].astype(F32)) * _dot(ga_ref[...], wa_ref[...])
    mix = mix + jax.nn.sigmoid(kb_ref[...].astype(F32)) * _dot(gb_ref[...], wb_ref[...])
    mix = mix + jax.nn.sigmoid(kc_ref[...].astype(F32)) * _dot(gc_ref[...], wc_ref[...])
    new = x_ref[...] + gt_ref[...] * _dot(mix.astype(BF16), wo_ref[...])
    if tail == "final":
        refs[13][...] = _rms(new, refs[12][...])
    else:
        g_ref, sc_ref, sh_ref, o_ref, h_ref = refs[12:]
        o_ref[...] = new
        h_ref[...] = _modulated_norm(new, g_ref[...], sc_ref[...], sh_ref[...]).astype(BF16)


def _merge_call(x, ga, gb, gc, p, mod, wa, wb, wc, wo, rows_per_mod, mod_row0, final_g=None, next_norm=None):
    rows, d = x.shape
    tm = _row_tile(rows, rows_per_mod, 256)

    def rowblk(width, colblk=0):
        return pl.BlockSpec((tm, width), lambda i: (i, colblk))

    def weight(w):
        return pl.BlockSpec(w.shape, lambda i: (0, 0), pipeline_mode=pl.Buffered(1))

    def mod_spec(which):
        return pl.BlockSpec((None, None, 1, d), lambda i: (mod_row0 + (i * tm) // rows_per_mod, which, 0, 0))

    vec = pl.BlockSpec((1, d), lambda i: (0, 0))
    in_specs = [rowblk(d), rowblk(Q_WIDTH), rowblk(Q_WIDTH), rowblk(Q_WIDTH),
                rowblk(d, GATES // d), rowblk(d, GATES // d + 1), rowblk(d, GATES // d + 2),
                mod_spec(2), weight(wa), weight(wb), weight(wc), weight(wo)]
    args = [x, ga, gb, gc, p, p, p, mod, wa, wb, wc, wo]
    if final_g is not None:
        in_specs.append(vec)
        args.append(final_g.reshape(1, d))
        out_specs, out_shape = rowblk(d), jax.ShapeDtypeStruct((rows, d), F32)
    else:
        next_g, next_mod = next_norm
        in_specs += [vec, mod_spec(1), mod_spec(0)]
        args += [next_g.reshape(1, d), next_mod, next_mod]
        out_specs = [rowblk(d), rowblk(d)]
        out_shape = [jax.ShapeDtypeStruct((rows, d), F32), jax.ShapeDtypeStruct((rows, d), BF16)]
    return pl.pallas_call(
        functools.partial(_merge_kernel, tail="final" if final_g is not None else "next"),
        grid=(rows // tm,),
        in_specs=in_specs,
        out_specs=out_specs,
        out_shape=out_shape,
        compiler_params=_params("parallel"),
        name="merge_out",
    )(*args)


def _rope_tables(n):
    rows = n // GRID_W
    row = jnp.repeat(jnp.arange(rows, dtype=F32), GRID_W)
    col = jnp.tile(jnp.arange(GRID_W, dtype=F32), rows)
    n_freq = HEAD_DIM // 4
    inv_freq = ROPE_THETA ** (-jnp.arange(n_freq, dtype=F32) / n_freq)
    ang = jnp.concatenate([row[:, None] * inv_freq, col[:, None] * inv_freq], axis=-1)
    ang = jnp.concatenate([ang, ang], axis=-1)
    sign = jnp.where(jnp.arange(HEAD_DIM) < HEAD_DIM // 2, -1.0, 1.0).astype(F32)
    return jnp.cos(ang), jnp.sin(ang) * sign


def kernel(x, c, ctx, c_ctx, ada_w, ada_b, norm_g, w_in, a_sink, b_q_norm, b_k_norm, c_lower_bound, c_out_norm,
           w_branch_a, w_branch_b, w_branch_c, w_out, final_norm_g):
    batch, n, d = x.shape
    m = ctx.shape[1]
    depth = ada_w.shape[0]
    assert batch + 1 <= MOD_ROWS
    cos, sin = _rope_tables(n)
    lb_all = jnp.cumsum(jax.nn.softmax(c_lower_bound.astype(F32), axis=0), axis=0)
    lb_all = lb_all - lb_all[0:1]

    cv = jnp.zeros((MOD_ROWS, d), F32).at[:batch].set(c).at[batch].set(c_ctx)
    mod = _ada_call(cv, ada_w, ada_b).reshape(depth, MOD_ROWS, 3, 1, d)

    x_l = x.reshape(batch * n, d)
    x_c = ctx.reshape(batch * m, d)
    h_c = h_l = None
    for l in range(depth):
        with_ctx = l < depth - 1
        norm = (norm_g[l], mod[l]) if l == 0 else None
        p_c, lf_c = _inproj_call(x_c if l == 0 else h_c, norm, w_in, l, lb_all[l], batch * m, batch,
                                 col_tiles=None if with_ctx else LAST_LAYER_CTX_TILES)
        p_l, lf_l = _inproj_call(x_l if l == 0 else h_l, norm, w_in, l, lb_all[l], n, 0)
        ga_c, ga_l = _attn_call("a", p_c, p_l, (a_sink[l],), cos, sin, batch, with_ctx)
        gb_c, gb_l = _attn_call("b", p_c, p_l, (b_q_norm[l].reshape(1, -1), b_k_norm[l].reshape(1, -1)),
                                cos, sin, batch, with_ctx)
        gc_c, gc_l = _hgrn_call(p_c, lf_c, p_l, lf_l, c_out_norm[l].reshape(1, -1), batch, with_ctx)
        ws = (w_branch_a[l].astype(BF16), w_branch_b[l].astype(BF16), w_branch_c[l].astype(BF16),
              w_out[l].astype(BF16))
        if with_ctx:
            nxt = (norm_g[l + 1], mod[l + 1])
            x_c, h_c = _merge_call(x_c, ga_c, gb_c, gc_c, p_c, mod[l], *ws, batch * m, batch, next_norm=nxt)
            x_l, h_l = _merge_call(x_l, ga_l, gb_l, gc_l, p_l, mod[l], *ws, n, 0, next_norm=nxt)
        else:
            x_l = _merge_call(x_l, ga_l, gb_l, gc_l, p_l, mod[l], *ws, n, 0, final_g=final_norm_g)
    return x_l.reshape(batch, n, d)
```

```python
import functools

import numpy as np
import jax
import jax.numpy as jnp
from jax import lax
from jax.experimental import pallas as pl
from jax.experimental.pallas import tpu as pltpu

F32 = jnp.float32
BF16 = jnp.bfloat16

HEAD_DIM = 128
A_HEADS, A_KV_HEADS = 8, 2
B_HEADS, B_KV_HEADS = 8, 2
C_HEADS = 8
WINDOW = 128
GRID_W = 64
ROPE_THETA = 10000.0
NORM_EPS = 1e-6
N_BRANCH = 3
GQA_GROUP = A_HEADS // A_KV_HEADS
Q_WIDTH = A_HEADS * HEAD_DIM
GROUP_WIDTH = GQA_GROUP * HEAD_DIM

A_Q, A_K, A_V, A_Z = 0, 1024, 1280, 1536
B_Q, B_K, B_V, B_Z = 2560, 3584, 3840, 4096
C_Q, C_ZF, C_ZB, C_I, C_Z = 5120, 6144, 7168, 8192, 9216
GATES = 10240
IN_PROJ_TM = 1024
IN_PROJ_TN = 1024
GATE_SLAB = 256
MERGE_TM = 256
ADA_TN = 512
HGRN_OUT_ROWS = 256
LAST_LAYER_CTX_TILES = tuple(sorted({c // IN_PROJ_TN for c in (A_K, A_V, B_K, B_V, C_Q, C_ZF, C_ZB, C_I)}))

ATTN_TQ = 256
HGRN_CHUNK = 64
HGRN_UNROLL = 9
MOD_ROWS = 16
LOG2E = 1.4426950408889634
NEG_BIG = -1e30
VMEM_LIMIT = 56 * 1024 * 1024


def _dot(a, b):
    return jnp.dot(a, b, preferred_element_type=F32)


def _dot_nt(a, b):
    return lax.dot_general(a, b, (((1,), (1,)), ((), ())), preferred_element_type=F32)


def _dot_tn(a, b):
    return lax.dot_general(a, b, (((0,), (0,)), ((), ())), preferred_element_type=F32)


def _rms(x, g):
    return x * lax.rsqrt(jnp.mean(x * x, axis=-1, keepdims=True) + NORM_EPS) * g


def _rope(x, cos, sin_signed):
    return x * cos + pltpu.roll(x, HEAD_DIM // 2, 1) * sin_signed


def _silu(x):
    return x * jax.nn.sigmoid(x)


def _params(*sem, flags=None):
    return pltpu.CompilerParams(dimension_semantics=sem, vmem_limit_bytes=VMEM_LIMIT, flags=flags)


def _ada_kernel(cv_ref, w_ref, b_ref, o_ref):
    s = _silu(cv_ref[...]).astype(BF16)
    o_ref[...] = _dot(s, w_ref[...].astype(BF16)) + b_ref[...]


def _ada_call(cv, ada_w, ada_b, tn=ADA_TN):
    depth, d, n3 = ada_w.shape
    return pl.pallas_call(
        _ada_kernel,
        grid=(depth, n3 // tn),
        in_specs=[
            pl.BlockSpec((MOD_ROWS, d), lambda l, j: (0, 0)),
            pl.BlockSpec((None, d, tn), lambda l, j: (l, 0, j)),
            pl.BlockSpec((None, 1, tn), lambda l, j: (l, 0, j)),
        ],
        out_specs=pl.BlockSpec((None, MOD_ROWS, tn), lambda l, j: (l, 0, j)),
        out_shape=jax.ShapeDtypeStruct((depth, MOD_ROWS, n3), F32),
        compiler_params=_params("parallel", "parallel"),
        name="ada_mod",
    )(cv, ada_w, ada_b.reshape(depth, 1, n3))


def _modulated_norm(x, g, sc, sh):
    return _rms(x, g) * (1.0 + sc) + sh


def _forget_gate(z, lb):
    log_lb, log1m_lb, one_m_lb = jnp.log(lb), jnp.log(1.0 - lb), 1.0 - lb
    u = jnp.exp(-jnp.abs(z))
    log_sig = jnp.minimum(z, 0.0) - jnp.log(1.0 + u)
    b = log1m_lb + log_sig
    g = jnp.maximum(log_lb, b) + jnp.log(1.0 + jnp.exp(-jnp.abs(log_lb - b)))
    kk = one_m_lb * jnp.where(z >= 0.0, u, 1.0) * (1.0 / (1.0 + u))
    return g, kk


def _inproj_kernel(*refs, pre_normed, tiles):
    if pre_normed:
        h_ref, w_ref, lb_ref, o_ref, lf_ref = refs
    else:
        x_ref, g_ref, sc_ref, sh_ref, w_ref, lb_ref, o_ref, lf_ref, h_ref = refs

        @pl.when(pl.program_id(1) == 0)
        def _():
            h_ref[...] = _modulated_norm(x_ref[...], g_ref[...], sc_ref[...], sh_ref[...]).astype(BF16)

    tile = pl.program_id(1) if tiles is None else _lookup(tiles, pl.program_id(1))
    is_gate = (tile == C_ZF // IN_PROJ_TN) | (tile == C_ZB // IN_PROJ_TN)

    @pl.when(jnp.logical_not(is_gate))
    def _():
        o_ref[...] = _dot(h_ref[...], w_ref[...].astype(BF16)).astype(BF16)

    @pl.when(is_gate)
    def _():
        acc = _dot(h_ref[...], w_ref[...].astype(BF16))
        for c0 in range(0, IN_PROJ_TN, GATE_SLAB):
            g, kk = _forget_gate(acc[:, c0:c0 + GATE_SLAB], lb_ref[:, c0:c0 + GATE_SLAB])
            o_ref[:, c0:c0 + GATE_SLAB] = kk.astype(BF16)
            lf_ref[:, c0:c0 + GATE_SLAB] = g


def _lookup(table, j):
    v = table[-1]
    for k in range(len(table) - 2, -1, -1):
        v = jnp.where(j == k, table[k], v)
    return v


def _row_tile(rows, rows_per_mod, cap):
    t = cap
    while rows % t or rows_per_mod % t:
        t //= 2
    return t


def _inproj_call(x, norm, w_all, layer, lb, rows_per_mod, mod_row0, col_tiles=None):
    rows, d = x.shape
    n = w_all.shape[2]
    tn = IN_PROJ_TN
    tm = _row_tile(rows, rows_per_mod, IN_PROJ_TM)
    tiles = tuple(range(n // tn)) if col_tiles is None else tuple(col_tiles)
    zb_tile = C_ZB // tn

    def col(j):
        return j if col_tiles is None else _lookup(tiles, j)

    def dirn(j):
        return jnp.where(col(j) >= zb_tile, 1, 0)

    in_specs = [pl.BlockSpec((tm, d), lambda i, j: (i, 0))]
    args = [x]
    scratch = []
    if norm is not None:
        g, mod = norm

        def mod_spec(which):
            return pl.BlockSpec((None, None, 1, d),
                                lambda i, j: (mod_row0 + (i * tm) // rows_per_mod, which, 0, 0))

        in_specs += [pl.BlockSpec((1, d), lambda i, j: (0, 0)), mod_spec(1), mod_spec(0)]
        args += [g.reshape(1, d), mod, mod]
        scratch = scratch + [pltpu.VMEM((tm, d), BF16)]
    in_specs += [pl.BlockSpec((None, d, tn), lambda i, j: (layer, 0, col(j))),
                 pl.BlockSpec((None, 1, tn), lambda i, j: (dirn(j), 0, 0))]
    args += [w_all, lb.reshape(2, 1, tn)]
    return pl.pallas_call(
        functools.partial(_inproj_kernel, pre_normed=norm is None, tiles=None if col_tiles is None else tiles),
        grid=(rows // tm, len(tiles)),
        in_specs=in_specs,
        out_specs=[pl.BlockSpec((tm, tn), lambda i, j: (i, col(j))),
                   pl.BlockSpec((tm, tn), lambda i, j: (i, dirn(j)))],
        out_shape=[jax.ShapeDtypeStruct((rows, n), BF16), jax.ShapeDtypeStruct((rows, 2 * tn), F32)],
        scratch_shapes=scratch,
        compiler_params=_params("parallel", "arbitrary"),
        name="in_proj",
    )(*args)


def _softmax_pv(parts, extra_logit):
    m = functools.reduce(jnp.maximum, [jnp.max(s, axis=-1, keepdims=True) for s, _ in parts])
    if extra_logit is not None:
        m = jnp.maximum(m, extra_logit)
    ps = [jnp.exp2(s - m) for s, _ in parts]
    l = functools.reduce(jnp.add, [jnp.sum(p, axis=-1, keepdims=True) for p in ps])
    if extra_logit is not None:
        l = l + jnp.exp2(extra_logit - m)
    o = functools.reduce(jnp.add, [_dot(p.astype(BF16), v) for p, (_, v) in zip(ps, parts)])
    return o * (1.0 / l)


def _head(ref, g):
    return ref[:, g * HEAD_DIM:(g + 1) * HEAD_DIM]


def _store_gated(o_ref, z_ref, g, o):
    z = _head(z_ref, g).astype(F32)
    o_ref[:, g * HEAD_DIM:(g + 1) * HEAD_DIM] = (o * _silu(z)).astype(BF16)


def _attn_a_kernel(*refs, with_ctx, n_lat, scale):
    if with_ctx:
        (sink_ref, qc_ref, zc_ref, ql_ref, zl_ref, kc_ref, vc_ref, kl_ref, vl_ref, cos_ref, sin_ref,
         oc_ref, ol_ref, kl_s) = refs
    else:
        (sink_ref, ql_ref, zl_ref, kc_ref, vc_ref, kl_ref, vl_ref, cos_ref, sin_ref, ol_ref, kl_s) = refs
    hk = pl.program_id(1)
    qi = pl.program_id(2)
    q0 = 1 if with_ctx else 0
    tq = ATTN_TQ
    win_k = tq + 2 * WINDOW

    @pl.when(qi == 0)
    def _prep():
        def body(i, carry):
            r = pl.multiple_of(i * tq, tq)
            k = kl_ref[pl.ds(r, tq), :].astype(F32)
            kl_s[pl.ds(r, tq), :] = _rope(k, cos_ref[pl.ds(r, tq), :], sin_ref[pl.ds(r, tq), :]).astype(BF16)
            return carry
        lax.fori_loop(0, n_lat // tq, body, 0)

    if with_ctx:
        @pl.when(qi == 0)
        def _ctx():
            for g in range(GQA_GROUP):
                s = _dot_nt(_head(qc_ref, g), kc_ref[...]) * scale
                o = _softmax_pv([(s, vc_ref[...])], sink_ref[hk * GQA_GROUP + g] * LOG2E)
                _store_gated(oc_ref, zc_ref, g, o)

    @pl.when(qi >= q0)
    def _lat():
        r0 = pl.multiple_of((qi - q0) * tq, tq)
        ws = pl.multiple_of(jnp.clip(r0 - WINDOW, 0, n_lat - win_k), WINDOW)
        kw = kl_s[pl.ds(ws, win_k), :]
        vw = vl_ref[pl.ds(ws, win_k), :]
        qpos = r0 + lax.broadcasted_iota(jnp.int32, (tq, win_k), 0)
        kpos = ws + lax.broadcasted_iota(jnp.int32, (tq, win_k), 1)
        valid = jnp.abs(qpos - kpos) <= WINDOW
        cos = cos_ref[pl.ds(r0, tq), :]
        sin = sin_ref[pl.ds(r0, tq), :]
        qs = [_rope(_head(ql_ref, g).astype(F32), cos, sin).astype(BF16) for g in range(GQA_GROUP)]
        scores = [(_dot_nt(q, kw), _dot_nt(q, kc_ref[...])) for q in qs]
        for g, (s_w, s_c) in enumerate(scores):
            s_w = jnp.where(valid, s_w * scale, NEG_BIG)
            o = _softmax_pv([(s_w, vw), (s_c * scale, vc_ref[...])], sink_ref[hk * GQA_GROUP + g] * LOG2E)
            _store_gated(ol_ref, zl_ref, g, o)


def _attn_b_kernel(*refs, with_ctx, n_lat, scale):
    if with_ctx:
        (qc_ref, zc_ref, ql_ref, zl_ref, kc_ref, vc_ref, kl_ref, vl_ref, cos_ref, sin_ref, qn_ref, kn_ref,
         oc_ref, ol_ref, kc_s, kl_s) = refs
    else:
        (ql_ref, zl_ref, kc_ref, vc_ref, kl_ref, vl_ref, cos_ref, sin_ref, qn_ref, kn_ref,
         ol_ref, kc_s, kl_s) = refs
    qi = pl.program_id(2)
    q0 = 1 if with_ctx else 0
    tq = ATTN_TQ

    @pl.when(qi == 0)
    def _prep():
        kn = kn_ref[...] * scale
        kc_s[...] = _rms(kc_ref[...].astype(F32), kn).astype(BF16)

        def body(i, carry):
            r = pl.multiple_of(i * tq, tq)
            k = _rms(kl_ref[pl.ds(r, tq), :].astype(F32), kn)
            kl_s[pl.ds(r, tq), :] = _rope(k, cos_ref[pl.ds(r, tq), :], sin_ref[pl.ds(r, tq), :]).astype(BF16)
            return carry
        lax.fori_loop(0, n_lat // tq, body, 0)

    if with_ctx:
        @pl.when(qi == 0)
        def _ctx():
            for g in range(GQA_GROUP):
                q = _rms(_head(qc_ref, g).astype(F32), qn_ref[...]).astype(BF16)
                o = _softmax_pv([(_dot_nt(q, kc_s[...]), vc_ref[...])], None)
                _store_gated(oc_ref, zc_ref, g, o)

    @pl.when(qi >= q0)
    def _lat():
        r0 = pl.multiple_of((qi - q0) * tq, tq)
        cos = cos_ref[pl.ds(r0, tq), :]
        sin = sin_ref[pl.ds(r0, tq), :]
        qs = [_rope(_rms(_head(ql_ref, g).astype(F32), qn_ref[...]), cos, sin).astype(BF16) for g in range(GQA_GROUP)]
        scores = [(_dot_nt(q, kc_s[...]), _dot_nt(q, kl_s[...])) for q in qs]
        for g, (s_c, s_l) in enumerate(scores):
            o = _softmax_pv([(s_c, vc_ref[...]), (s_l, vl_ref[...])], None)
            _store_gated(ol_ref, zl_ref, g, o)


def _attn_call(kind, p_c, p_l, extras, cos, sin, batch, with_ctx):
    m = p_c.shape[0] // batch
    n = p_l.shape[0] // batch
    tq = ATTN_TQ
    assert m == tq and n % tq == 0 and n >= tq + 2 * WINDOW
    nq = n // tq
    q0 = 1 if with_ctx else 0
    if kind == "a":
        qcol, kcol, vcol, zcol = A_Q, A_K, A_V, A_Z
    else:
        qcol, kcol, vcol, zcol = B_Q, B_K, B_V, B_Z
    gw, hd = GROUP_WIDTH, HEAD_DIM

    def lat_row(b, hk, qi):
        return b * nq + jnp.maximum(qi - q0, 0)

    def grp_c(col):
        return pl.BlockSpec((tq, gw), lambda b, hk, qi: (b, col // gw + hk))

    def grp_l(col):
        return pl.BlockSpec((tq, gw), lambda b, hk, qi: (lat_row(b, hk, qi), col // gw + hk))

    def kv(rows, col):
        return pl.BlockSpec((rows, hd), lambda b, hk, qi: (b, col // hd + hk))

    def full(a):
        return pl.BlockSpec(a.shape, lambda b, hk, qi: (0,) * a.ndim)

    in_specs, args = [], []
    if kind == "a":
        in_specs.append(pl.BlockSpec(memory_space=pltpu.SMEM))
        args.append(extras[0])
    if with_ctx:
        in_specs += [grp_c(qcol), grp_c(zcol)]
        args += [p_c, p_c]
    in_specs += [grp_l(qcol), grp_l(zcol), kv(m, kcol), kv(m, vcol), kv(n, kcol), kv(n, vcol), full(cos), full(sin)]
    args += [p_l, p_l, p_c, p_c, p_l, p_l, cos, sin]
    scratch = [pltpu.VMEM((n, hd), BF16)]
    if kind == "b":
        in_specs += [full(extras[0]), full(extras[1])]
        args += list(extras)
        scratch = [pltpu.VMEM((m, hd), BF16)] + scratch

    out_l_spec = pl.BlockSpec((tq, gw), lambda b, hk, qi: (lat_row(b, hk, qi), hk))
    out_l_shape = jax.ShapeDtypeStruct((batch * n, Q_WIDTH), BF16)
    if with_ctx:
        out_specs = [pl.BlockSpec((tq, gw), lambda b, hk, qi: (b, hk)), out_l_spec]
        out_shape = [jax.ShapeDtypeStruct((batch * m, Q_WIDTH), BF16), out_l_shape]
    else:
        out_specs, out_shape = [out_l_spec], [out_l_shape]

    body = _attn_a_kernel if kind == "a" else _attn_b_kernel
    outs = pl.pallas_call(
        functools.partial(body, with_ctx=with_ctx, n_lat=n, scale=HEAD_DIM ** -0.5 * LOG2E),
        grid=(batch, A_KV_HEADS, nq + q0),
        in_specs=in_specs,
        out_specs=out_specs,
        out_shape=out_shape,
        scratch_shapes=scratch,
        compiler_params=_params("parallel", "parallel", "arbitrary"),
        name="attn_" + kind,
    )(*args)
    return (outs[0], outs[1]) if with_ctx else (None, outs[0])


HGRN_PAIRS = ((0, 32), (1, 2), (4, 8), (16, None))
HGRN_SELECT_LEVELS = (1, 2, 4, 8)
HGRN_SIGN_LEVELS = (1, 2, 4)


def _hgrn_constants():
    c = HGRN_CHUNK
    t = np.arange(c)[:, None]
    s = np.arange(c)[None, :]
    tri = s <= t

    def pair_mask(h):
        if h is None:
            return np.zeros((c, c), bool)
        if h == 0:
            return t == s
        blk = 2 * h
        return (t // blk == s // blk) & (t % blk >= h) & (s % blk < h)

    mw = np.stack([np.stack([np.concatenate([pair_mask(a).T if rev else pair_mask(a),
                                             pair_mask(b).T if rev else pair_mask(b)], axis=1)
                             for a, b in HGRN_PAIRS]) for rev in (False, True)])
    rows = np.broadcast_to(np.arange(c)[:, None], (c, HEAD_DIM))

    def t_role(h, rev):
        ph = rows % (2 * h)
        return (ph < h) if rev else (ph >= h)

    role = np.stack([np.stack([t_role(h, rev) for h in HGRN_SELECT_LEVELS]) for rev in (False, True)])
    sgn = np.stack([np.stack([np.where(t_role(h, rev), 1.0, -1.0) for h in HGRN_SIGN_LEVELS])
                    for rev in (False, True)])
    return (jnp.asarray(np.stack([tri, tri.T]), dtype=BF16), jnp.asarray(mw, dtype=BF16),
            jnp.asarray(role, dtype=BF16), jnp.asarray(sgn, dtype=F32))


def _hgrn_cumulate(g, tri):
    g_hi = g.astype(BF16)
    g_lo = (g - g_hi.astype(F32)).astype(BF16)
    cc = _dot(tri, jnp.concatenate([g_hi, g_lo], axis=1))
    return (cc[:, :HEAD_DIM] + cc[:, HEAD_DIM:]) * LOG2E


def _hgrn_state_operands(q, kk, cum_ref, r, rev):
    c = HGRN_CHUNK
    cum = cum_ref[pl.ds(r, c), :]
    tot = cum_ref[pl.ds(r + (0 if rev else c - 1), 1), :]
    return q * jnp.exp2(cum).astype(BF16), kk * jnp.exp2(tot - cum).astype(BF16), tot


def _hgrn_pair_operands(q, kk, cum_ref, r, role_ref, sgn_ref, rev):
    c = HGRN_CHUNK
    cum = cum_ref[pl.ds(r, c), :]

    def row(k):
        return cum_ref[pl.ds(r + k, 1), :]

    sub = lax.broadcasted_iota(jnp.int32, (8, HEAD_DIM), 0)

    def operand(h):
        blk = 2 * h
        if h >= 8:
            d_parts, o_parts = [], []
            for b in range(c // blk):
                lo, mid, hi = b * blk, b * blk + h, (b + 1) * blk
                ref = row(mid if rev else mid - 1)
                if rev:
                    d_parts += [cum[lo:mid] - ref, ref - cum[mid:hi]]
                    o_parts += [q[lo:mid], kk[mid:hi]]
                else:
                    d_parts += [ref - cum[lo:mid], cum[mid:hi] - ref]
                    o_parts += [kk[lo:mid], q[mid:hi]]
            d = jnp.concatenate(d_parts, axis=0)
            if h >= 16:
                return jnp.concatenate(o_parts, axis=0) * jnp.exp2(d).astype(BF16)
        else:
            parts = []
            for grp in range(c // 8):
                base = grp * 8
                if h == 4:
                    ref = row(base + (4 if rev else 3))
                elif h == 2:
                    ref = jnp.where(sub < 4, row(base + (2 if rev else 1)), row(base + (6 if rev else 5)))
                else:
                    k0 = base + (1 if rev else 0)
                    ref = jnp.where(sub < 2, row(k0), jnp.where(sub < 4, row(k0 + 2),
                                                               jnp.where(sub < 6, row(k0 + 4), row(k0 + 6))))
                parts.append(cum[base:base + 8] - ref)
            d = jnp.concatenate(parts, axis=0) * sgn_ref[HGRN_SIGN_LEVELS.index(h)]
        opnd = jnp.where(role_ref[HGRN_SELECT_LEVELS.index(h)] > 0, q, kk)
        return opnd * jnp.exp2(d).astype(BF16)

    def sides(item):
        if item == 0:
            return q, kk
        z = operand(item)
        return z, z

    zeros = jnp.zeros((c, HEAD_DIM), BF16)
    pairs = []
    for a, b in HGRN_PAIRS:
        la, ra = sides(a)
        if b is None:
            pairs.append((la, jnp.concatenate([ra, zeros], axis=0)))
        else:
            lb, rb = sides(b)
            pairs.append((jnp.concatenate([la, lb], axis=1),
                          jnp.concatenate([jnp.concatenate([ra, zeros], axis=1),
                                           jnp.concatenate([zeros, rb], axis=1)], axis=0)))
    return pairs


def _hgrn_blocks(x):
    return x[:HGRN_CHUNK, :HEAD_DIM], x[HGRN_CHUNK:, HEAD_DIM:]


def _hgrn_intra_products(pairs_f, pairs_b):
    out = [_hgrn_blocks(_dot_nt(jnp.concatenate([pf[0], pb[0]], axis=0), jnp.concatenate([pf[1], pb[1]], axis=0)))
           for pf, pb in zip(pairs_f, pairs_b)]
    return [o[0] for o in out], [o[1] for o in out]


def _hgrn_intra_attn(products, mw_ref, d):
    acc = None
    for j, t in enumerate(products):
        t = t.astype(BF16) * mw_ref[d, j]
        acc = t if acc is None else acc + t
    return acc


def _hgrn_outputs(fwd, bwd):
    (qf, kf, tf), af, vf, sf = fwd
    (qb, kb, tb), ab, vb, sb = bwd
    st_f, st_b = sf[...], sb[...]
    o_f, o_b = _hgrn_blocks(_dot_nt(jnp.concatenate([qf, qb], axis=0),
                                    jnp.concatenate([st_f.astype(BF16), st_b.astype(BF16)], axis=0)))
    sf[...] = st_f * jnp.exp2(tf) + _dot_tn(vf, kf)
    sb[...] = st_b * jnp.exp2(tb) + _dot_tn(vb, kb)
    i_f, i_b = _hgrn_blocks(_dot(jnp.concatenate([af, ab], axis=0),
                                 jnp.concatenate([jnp.concatenate([vf, vf], axis=0),
                                                  jnp.concatenate([vb, vb], axis=0)], axis=1)))
    return o_f + i_f, o_b + i_b


def _hgrn_kernel(*refs, with_ctx, m_ctx, n_lat):
    (qc_ref, kfc_ref, kbc_ref, ic_ref, gfc_ref, gbc_ref,
     ql_ref, kfl_ref, kbl_ref, il_ref, gfl_ref, gbl_ref) = refs[:12]
    refs = refs[12:]
    if with_ctx:
        zc_ref, refs = refs[0], refs[1:]
    zl_ref, gain_ref, tri_ref, mw_ref, role_ref, sgn_ref = refs[:6]
    refs = refs[6:]
    if with_ctx:
        oc_ref, refs = refs[0], refs[1:]
    ol_ref = refs[0]
    st_f, st_b, q_s, v_s, kkf_s, kkb_s, cumf_s, cumb_s, of_s, ob_s = refs[1:]
    chunk = HGRN_CHUNK
    nc_c, nc_l = m_ctx // chunk, n_lat // chunk

    def gather(q_ref, kf_ref, kb_ref, i_ref, gf_ref, gb_ref, n_chunks, base):
        def body(c, carry):
            r = pl.multiple_of(c * chunk, chunk)
            src = pl.ds(r, chunk)
            dst = pl.ds(pl.multiple_of(base + r, chunk), chunk)
            q_s[dst, :] = q_ref[src, :]
            v_s[dst, :] = i_ref[src, :]
            kkf_s[dst, :] = kf_ref[src, :]
            kkb_s[dst, :] = kb_ref[src, :]
            cumf_s[dst, :] = _hgrn_cumulate(gf_ref[src, :], tri_ref[0])
            cumb_s[dst, :] = _hgrn_cumulate(gb_ref[src, :], tri_ref[1])
            return carry
        lax.fori_loop(0, n_chunks, body, 0, unroll=8 if n_chunks % 8 == 0 else 4)

    gather(qc_ref, kfc_ref, kbc_ref, ic_ref, gfc_ref, gbc_ref, nc_c, 0)
    gather(ql_ref, kfl_ref, kbl_ref, il_ref, gfl_ref, gbl_ref, nc_l, m_ctx)

    st_f[...] = jnp.zeros_like(st_f)
    st_b[...] = jnp.zeros_like(st_b)
    n_chunks = nc_c + nc_l

    def rows(c):
        cb = jnp.where(c < nc_c, nc_c - 1 - c, n_chunks - 1 - (c - nc_c))
        return pl.multiple_of(c * chunk, chunk), pl.multiple_of(cb * chunk, chunk)

    def products(c):
        rf, rb = rows(c)
        return _hgrn_intra_products(
            _hgrn_pair_operands(q_s[pl.ds(rf, chunk), :], kkf_s[pl.ds(rf, chunk), :], cumf_s, rf,
                                role_ref.at[0], sgn_ref.at[0], False),
            _hgrn_pair_operands(q_s[pl.ds(rb, chunk), :], kkb_s[pl.ds(rb, chunk), :], cumb_s, rb,
                                role_ref.at[1], sgn_ref.at[1], True))

    def attn(prods):
        return _hgrn_intra_attn(prods[0], mw_ref, 0), _hgrn_intra_attn(prods[1], mw_ref, 1)

    def body(c, carry):
        attn_f, attn_b = carry
        nxt = products(jnp.minimum(c + 1, n_chunks - 1))
        rf, rb = rows(c)
        of_s[pl.ds(rf, chunk), :], ob_s[pl.ds(rb, chunk), :] = _hgrn_outputs(
            (_hgrn_state_operands(q_s[pl.ds(rf, chunk), :], kkf_s[pl.ds(rf, chunk), :], cumf_s, rf, False),
             attn_f, v_s[pl.ds(rf, chunk), :], st_f),
            (_hgrn_state_operands(q_s[pl.ds(rb, chunk), :], kkb_s[pl.ds(rb, chunk), :], cumb_s, rb, True),
             attn_b, v_s[pl.ds(rb, chunk), :], st_b))
        return attn(nxt)
    lax.fori_loop(0, n_chunks, body, attn(products(0)),
                  unroll=HGRN_UNROLL if n_chunks % HGRN_UNROLL == 0 else 1)

    def finish(z_ref, o_ref, rows, base):
        blk = HGRN_OUT_ROWS

        def body(i, carry):
            r = pl.multiple_of(i * blk, blk)
            src = pl.ds(pl.multiple_of(base + r, blk), blk)
            o = _rms((of_s[src, :] + ob_s[src, :]) * (HEAD_DIM ** -0.5), gain_ref[...])
            o_ref[pl.ds(r, blk), :] = (o * _silu(z_ref[pl.ds(r, blk), :].astype(F32))).astype(BF16)
            return carry
        lax.fori_loop(0, rows // blk, body, 0, unroll=2 if (rows // blk) % 2 == 0 else 1)

    if with_ctx:
        finish(zc_ref, oc_ref, m_ctx, 0)
    finish(zl_ref, ol_ref, n_lat, m_ctx)


def _hgrn_call(p_c, lf_c, p_l, lf_l, gain, batch, with_ctx):
    m = p_c.shape[0] // batch
    n = p_l.shape[0] // batch
    hd = HEAD_DIM
    assert m % 256 == 0 and n % 256 == 0
    consts = _hgrn_constants()

    def col(rows, c):
        return pl.BlockSpec((rows, hd), lambda b, h: (b, c // hd + h))

    def full(a):
        return pl.BlockSpec(a.shape, lambda b, h: (0,) * a.ndim)

    def side(rows, p, lf):
        return ([col(rows, C_Q), col(rows, C_ZF), col(rows, C_ZB), col(rows, C_I), col(rows, 0), col(rows, C_HEADS * hd)],
                [p, p, p, p, lf, lf])

    specs_c, args_c = side(m, p_c, lf_c)
    specs_l, args_l = side(n, p_l, lf_l)
    in_specs, args = specs_c + specs_l, args_c + args_l
    if with_ctx:
        in_specs.append(col(m, C_Z))
        args.append(p_c)
    in_specs += [col(n, C_Z), full(gain)] + [full(a) for a in consts]
    args += [p_l, gain, *consts]

    out_l_spec = pl.BlockSpec((n, hd), lambda b, h: (b, h))
    out_l_shape = jax.ShapeDtypeStruct((batch * n, Q_WIDTH), BF16)
    if with_ctx:
        out_specs = [pl.BlockSpec((m, hd), lambda b, h: (b, h)), out_l_spec]
        out_shape = [jax.ShapeDtypeStruct((batch * m, Q_WIDTH), BF16), out_l_shape]
    else:
        out_specs, out_shape = [out_l_spec], [out_l_shape]

    outs = pl.pallas_call(
        functools.partial(_hgrn_kernel, with_ctx=with_ctx, m_ctx=m, n_lat=n),
        grid=(batch, C_HEADS),
        in_specs=in_specs,
        out_specs=out_specs,
        out_shape=out_shape,
        scratch_shapes=([pltpu.VMEM((hd, hd), F32)] * 2 + [pltpu.VMEM((m + n, hd), BF16)] * 4
                        + [pltpu.VMEM((m + n, hd), F32)] * 4),
        compiler_params=_params("parallel", "parallel"),
        name="hgrn2",
    )(*args)
    return (outs[0], outs[1]) if with_ctx else (None, outs[0])


def _merge_kernel(*refs, tail):
    (x_ref, ga_ref, gb_ref, gc_ref, ka_ref, kb_ref, kc_ref, gt_ref, wa_ref, wb_ref, wc_ref, wo_ref) = refs[:12]
    mix = jax.nn.sigmoid(ka_ref[...].astype(F32)) * _dot(ga_ref[...], wa_ref[...])
    mix = mix + jax.nn.sigmoid(kb_ref[...].astype(F32)) * _dot(gb_ref[...], wb_ref[...])
    mix = mix + jax.nn.sigmoid(kc_ref[...].astype(F32)) * _dot(gc_ref[...], wc_ref[...])
    new = x_ref[...] + gt_ref[...] * _dot(mix.astype(BF16), wo_ref[...])
    if tail == "final":
        refs[13][...] = _rms(new, refs[12][...])
    else:
        g_ref, sc_ref, sh_ref, o_ref, h_ref = refs[12:]
        o_ref[...] = new
        h_ref[...] = _modulated_norm(new, g_ref[...], sc_ref[...], sh_ref[...]).astype(BF16)


def _merge_call(x, ga, gb, gc, p, mod, wa, wb, wc, wo, rows_per_mod, mod_row0, final_g=None, next_norm=None):
    rows, d = x.shape
    tm = _row_tile(rows, rows_per_mod, MERGE_TM)

    def rowblk(width, colblk=0):
        return pl.BlockSpec((tm, width), lambda i: (i, colblk))

    def weight(w):
        return pl.BlockSpec(w.shape, lambda i: (0, 0), pipeline_mode=pl.Buffered(1))

    def mod_spec(which):
        return pl.BlockSpec((None, None, 1, d), lambda i: (mod_row0 + (i * tm) // rows_per_mod, which, 0, 0))

    vec = pl.BlockSpec((1, d), lambda i: (0, 0))
    in_specs = [rowblk(d), rowblk(Q_WIDTH), rowblk(Q_WIDTH), rowblk(Q_WIDTH),
                rowblk(d, GATES // d), rowblk(d, GATES // d + 1), rowblk(d, GATES // d + 2),
                mod_spec(2), weight(wa), weight(wb), weight(wc), weight(wo)]
    args = [x, ga, gb, gc, p, p, p, mod, wa, wb, wc, wo]
    if final_g is not None:
        in_specs.append(vec)
        args.append(final_g.reshape(1, d))
        out_specs, out_shape = rowblk(d), jax.ShapeDtypeStruct((rows, d), F32)
    else:
        next_g, next_mod = next_norm
        in_specs += [vec, mod_spec(1), mod_spec(0)]
        args += [next_g.reshape(1, d), next_mod, next_mod]
        out_specs = [rowblk(d), rowblk(d)]
        out_shape = [jax.ShapeDtypeStruct((rows, d), F32), jax.ShapeDtypeStruct((rows, d), BF16)]
    return pl.pallas_call(
        functools.partial(_merge_kernel, tail="final" if final_g is not None else "next"),
        grid=(rows // tm,),
        in_specs=in_specs,
        out_specs=out_specs,
        out_shape=out_shape,
        compiler_params=_params("parallel"),
        name="merge_out",
    )(*args)


def _rope_tables(n):
    rows = n // GRID_W
    row = jnp.repeat(jnp.arange(rows, dtype=F32), GRID_W)
    col = jnp.tile(jnp.arange(GRID_W, dtype=F32), rows)
    n_freq = HEAD_DIM // 4
    inv_freq = ROPE_THETA ** (-jnp.arange(n_freq, dtype=F32) / n_freq)
    ang = jnp.concatenate([row[:, None] * inv_freq, col[:, None] * inv_freq], axis=-1)
    ang = jnp.concatenate([ang, ang], axis=-1)
    sign = jnp.where(jnp.arange(HEAD_DIM) < HEAD_DIM // 2, -1.0, 1.0).astype(F32)
    return jnp.cos(ang), jnp.sin(ang) * sign


def kernel(x, c, ctx, c_ctx, ada_w, ada_b, norm_g, w_in, a_sink, b_q_norm, b_k_norm, c_lower_bound, c_out_norm,
           w_branch_a, w_branch_b, w_branch_c, w_out, final_norm_g):
    batch, n, d = x.shape
    m = ctx.shape[1]
    depth = ada_w.shape[0]
    assert batch + 1 <= MOD_ROWS
    cos, sin = _rope_tables(n)
    lb_all = jnp.cumsum(jax.nn.softmax(c_lower_bound.astype(F32), axis=0), axis=0)
    lb_all = lb_all - lb_all[0:1]

    cv = jnp.zeros((MOD_ROWS, d), F32).at[:batch].set(c).at[batch].set(c_ctx)
    mod = _ada_call(cv, ada_w, ada_b).reshape(depth, MOD_ROWS, 3, 1, d)

    x_l = x.reshape(batch * n, d)
    x_c = ctx.reshape(batch * m, d)
    h_c = h_l = None
    for l in range(depth):
        with_ctx = l < depth - 1
        norm = (norm_g[l], mod[l]) if l == 0 else None
        p_c, lf_c = _inproj_call(x_c if l == 0 else h_c, norm, w_in, l, lb_all[l], batch * m, batch,
                                 col_tiles=None if with_ctx else LAST_LAYER_CTX_TILES)
        p_l, lf_l = _inproj_call(x_l if l == 0 else h_l, norm, w_in, l, lb_all[l], n, 0)
        ga_c, ga_l = _attn_call("a", p_c, p_l, (a_sink[l],), cos, sin, batch, with_ctx)
        gb_c, gb_l = _attn_call("b", p_c, p_l, (b_q_norm[l].reshape(1, -1), b_k_norm[l].reshape(1, -1)),
                                cos, sin, batch, with_ctx)
        gc_c, gc_l = _hgrn_call(p_c, lf_c, p_l, lf_l, c_out_norm[l].reshape(1, -1), batch, with_ctx)
        ws = (w_branch_a[l].astype(BF16), w_branch_b[l].astype(BF16), w_branch_c[l].astype(BF16),
              w_out[l].astype(BF16))
        if with_ctx:
            nxt = (norm_g[l + 1], mod[l + 1])
            x_c, h_c = _merge_call(x_c, ga_c, gb_c, gc_c, p_c, mod[l], *ws, batch * m, batch, next_norm=nxt)
            x_l, h_l = _merge_call(x_l, ga_l, gb_l, gc_l, p_l, mod[l], *ws, n, 0, next_norm=nxt)
        else:
            x_l = _merge_call(x_l, ga_l, gb_l, gc_l, p_l, mod[l], *ws, n, 0, final_g=final_norm_g)
    return x_l.reshape(batch, n, d)
```

```python
import functools

import numpy as np
import jax
import jax.numpy as jnp
from jax import lax
from jax.experimental import pallas as pl
from jax.experimental.pallas import tpu as pltpu

F32 = jnp.float32
BF16 = jnp.bfloat16

HEAD_DIM = 128
A_HEADS, A_KV_HEADS = 8, 2
B_HEADS, B_KV_HEADS = 8, 2
C_HEADS = 8
WINDOW = 128
GRID_W = 64
ROPE_THETA = 10000.0
NORM_EPS = 1e-6
N_BRANCH = 3
GQA_GROUP = A_HEADS // A_KV_HEADS
Q_WIDTH = A_HEADS * HEAD_DIM
GROUP_WIDTH = GQA_GROUP * HEAD_DIM

A_Q, A_K, A_V, A_Z = 0, 1024, 1280, 1536
B_Q, B_K, B_V, B_Z = 2560, 3584, 3840, 4096
C_Q, C_ZF, C_ZB, C_I, C_Z = 5120, 6144, 7168, 8192, 9216
GATES = 10240
IN_PROJ_TM = 1024
IN_PROJ_TN = 1024
GATE_SLAB = 256
MERGE_TM = 256
ADA_TN = 512
HGRN_OUT_ROWS = 256
LAST_LAYER_CTX_TILES = tuple(sorted({c // IN_PROJ_TN for c in (A_K, A_V, B_K, B_V, C_Q, C_ZF, C_ZB, C_I)}))

ATTN_TQ = 256
HGRN_CHUNK = 64
HGRN_UNROLL = 18
MOD_ROWS = 16
LOG2E = 1.4426950408889634
NEG_BIG = -1e30
VMEM_LIMIT = 56 * 1024 * 1024


def _dot(a, b):
    return jnp.dot(a, b, preferred_element_type=F32)


def _dot_nt(a, b):
    return lax.dot_general(a, b, (((1,), (1,)), ((), ())), preferred_element_type=F32)


def _dot_tn(a, b):
    return lax.dot_general(a, b, (((0,), (0,)), ((), ())), preferred_element_type=F32)


def _rms(x, g):
    return x * lax.rsqrt(jnp.mean(x * x, axis=-1, keepdims=True) + NORM_EPS) * g


def _rope(x, cos, sin_signed):
    return x * cos + pltpu.roll(x, HEAD_DIM // 2, 1) * sin_signed


def _silu(x):
    return x * jax.nn.sigmoid(x)


def _params(*sem, flags=None):
    return pltpu.CompilerParams(dimension_semantics=sem, vmem_limit_bytes=VMEM_LIMIT, flags=flags)


def _ada_kernel(cv_ref, w_ref, b_ref, o_ref):
    s = _silu(cv_ref[...]).astype(BF16)
    o_ref[...] = _dot(s, w_ref[...].astype(BF16)) + b_ref[...]


def _ada_call(cv, ada_w, ada_b, tn=ADA_TN):
    depth, d, n3 = ada_w.shape
    return pl.pallas_call(
        _ada_kernel,
        grid=(depth, n3 // tn),
        in_specs=[
            pl.BlockSpec((MOD_ROWS, d), lambda l, j: (0, 0)),
            pl.BlockSpec((None, d, tn), lambda l, j: (l, 0, j)),
            pl.BlockSpec((None, 1, tn), lambda l, j: (l, 0, j)),
        ],
        out_specs=pl.BlockSpec((None, MOD_ROWS, tn), lambda l, j: (l, 0, j)),
        out_shape=jax.ShapeDtypeStruct((depth, MOD_ROWS, n3), F32),
        compiler_params=_params("parallel", "parallel"),
        name="ada_mod",
    )(cv, ada_w, ada_b.reshape(depth, 1, n3))


def _modulated_norm(x, g, sc, sh):
    return _rms(x, g) * (1.0 + sc) + sh


def _forget_gate(z, lb):
    log_lb, log1m_lb, one_m_lb = jnp.log(lb), jnp.log(1.0 - lb), 1.0 - lb
    u = jnp.exp(-jnp.abs(z))
    log_sig = jnp.minimum(z, 0.0) - jnp.log(1.0 + u)
    b = log1m_lb + log_sig
    g = jnp.maximum(log_lb, b) + jnp.log(1.0 + jnp.exp(-jnp.abs(log_lb - b)))
    kk = one_m_lb * jnp.where(z >= 0.0, u, 1.0) * (1.0 / (1.0 + u))
    return g, kk


def _inproj_kernel(*refs, pre_normed, tiles):
    if pre_normed:
        h_ref, w_ref, lb_ref, o_ref, lf_ref = refs
    else:
        x_ref, g_ref, sc_ref, sh_ref, w_ref, lb_ref, o_ref, lf_ref, h_ref = refs

        @pl.when(pl.program_id(1) == 0)
        def _():
            h_ref[...] = _modulated_norm(x_ref[...], g_ref[...], sc_ref[...], sh_ref[...]).astype(BF16)

    tile = pl.program_id(1) if tiles is None else _lookup(tiles, pl.program_id(1))
    is_gate = (tile == C_ZF // IN_PROJ_TN) | (tile == C_ZB // IN_PROJ_TN)

    @pl.when(jnp.logical_not(is_gate))
    def _():
        o_ref[...] = _dot(h_ref[...], w_ref[...].astype(BF16)).astype(BF16)

    @pl.when(is_gate)
    def _():
        acc = _dot(h_ref[...], w_ref[...].astype(BF16))
        for c0 in range(0, IN_PROJ_TN, GATE_SLAB):
            g, kk = _forget_gate(acc[:, c0:c0 + GATE_SLAB], lb_ref[:, c0:c0 + GATE_SLAB])
            o_ref[:, c0:c0 + GATE_SLAB] = kk.astype(BF16)
            lf_ref[:, c0:c0 + GATE_SLAB] = g


def _lookup(table, j):
    v = table[-1]
    for k in range(len(table) - 2, -1, -1):
        v = jnp.where(j == k, table[k], v)
    return v


def _row_tile(rows, rows_per_mod, cap):
    t = cap
    while rows % t or rows_per_mod % t:
        t //= 2
    return t


def _inproj_call(x, norm, w_all, layer, lb, rows_per_mod, mod_row0, col_tiles=None):
    rows, d = x.shape
    n = w_all.shape[2]
    tn = IN_PROJ_TN
    tm = _row_tile(rows, rows_per_mod, IN_PROJ_TM)
    tiles = tuple(range(n // tn)) if col_tiles is None else tuple(col_tiles)
    zb_tile = C_ZB // tn

    def col(j):
        return j if col_tiles is None else _lookup(tiles, j)

    def dirn(j):
        return jnp.where(col(j) >= zb_tile, 1, 0)

    in_specs = [pl.BlockSpec((tm, d), lambda i, j: (i, 0))]
    args = [x]
    scratch = []
    if norm is not None:
        g, mod = norm

        def mod_spec(which):
            return pl.BlockSpec((None, None, 1, d),
                                lambda i, j: (mod_row0 + (i * tm) // rows_per_mod, which, 0, 0))

        in_specs += [pl.BlockSpec((1, d), lambda i, j: (0, 0)), mod_spec(1), mod_spec(0)]
        args += [g.reshape(1, d), mod, mod]
        scratch = scratch + [pltpu.VMEM((tm, d), BF16)]
    in_specs += [pl.BlockSpec((None, d, tn), lambda i, j: (layer, 0, col(j))),
                 pl.BlockSpec((None, 1, tn), lambda i, j: (dirn(j), 0, 0))]
    args += [w_all, lb.reshape(2, 1, tn)]
    return pl.pallas_call(
        functools.partial(_inproj_kernel, pre_normed=norm is None, tiles=None if col_tiles is None else tiles),
        grid=(rows // tm, len(tiles)),
        in_specs=in_specs,
        out_specs=[pl.BlockSpec((tm, tn), lambda i, j: (i, col(j))),
                   pl.BlockSpec((tm, tn), lambda i, j: (i, dirn(j)))],
        out_shape=[jax.ShapeDtypeStruct((rows, n), BF16), jax.ShapeDtypeStruct((rows, 2 * tn), F32)],
        scratch_shapes=scratch,
        compiler_params=_params("parallel", "arbitrary"),
        name="in_proj",
    )(*args)


def _softmax_pv(parts, extra_logit):
    m = functools.reduce(jnp.maximum, [jnp.max(s, axis=-1, keepdims=True) for s, _ in parts])
    if extra_logit is not None:
        m = jnp.maximum(m, extra_logit)
    ps = [jnp.exp2(s - m) for s, _ in parts]
    l = functools.reduce(jnp.add, [jnp.sum(p, axis=-1, keepdims=True) for p in ps])
    if extra_logit is not None:
        l = l + jnp.exp2(extra_logit - m)
    o = functools.reduce(jnp.add, [_dot(p.astype(BF16), v) for p, (_, v) in zip(ps, parts)])
    return o * (1.0 / l)


def _head(ref, g):
    return ref[:, g * HEAD_DIM:(g + 1) * HEAD_DIM]


def _store_gated(o_ref, z_ref, g, o):
    z = _head(z_ref, g).astype(F32)
    o_ref[:, g * HEAD_DIM:(g + 1) * HEAD_DIM] = (o * _silu(z)).astype(BF16)


def _attn_a_kernel(*refs, with_ctx, n_lat, scale):
    if with_ctx:
        (sink_ref, qc_ref, zc_ref, ql_ref, zl_ref, kc_ref, vc_ref, kl_ref, vl_ref, cos_ref, sin_ref,
         oc_ref, ol_ref, kl_s) = refs
    else:
        (sink_ref, ql_ref, zl_ref, kc_ref, vc_ref, kl_ref, vl_ref, cos_ref, sin_ref, ol_ref, kl_s) = refs
    hk = pl.program_id(1)
    qi = pl.program_id(2)
    q0 = 1 if with_ctx else 0
    tq = ATTN_TQ
    win_k = tq + 2 * WINDOW

    @pl.when(qi == 0)
    def _prep():
        def body(i, carry):
            r = pl.multiple_of(i * tq, tq)
            k = kl_ref[pl.ds(r, tq), :].astype(F32)
            kl_s[pl.ds(r, tq), :] = _rope(k, cos_ref[pl.ds(r, tq), :], sin_ref[pl.ds(r, tq), :]).astype(BF16)
            return carry
        lax.fori_loop(0, n_lat // tq, body, 0)

    if with_ctx:
        @pl.when(qi == 0)
        def _ctx():
            for g in range(GQA_GROUP):
                s = _dot_nt(_head(qc_ref, g), kc_ref[...]) * scale
                o = _softmax_pv([(s, vc_ref[...])], sink_ref[hk * GQA_GROUP + g] * LOG2E)
                _store_gated(oc_ref, zc_ref, g, o)

    @pl.when(qi >= q0)
    def _lat():
        r0 = pl.multiple_of((qi - q0) * tq, tq)
        ws = pl.multiple_of(jnp.clip(r0 - WINDOW, 0, n_lat - win_k), WINDOW)
        kw = kl_s[pl.ds(ws, win_k), :]
        vw = vl_ref[pl.ds(ws, win_k), :]
        qpos = r0 + lax.broadcasted_iota(jnp.int32, (tq, win_k), 0)
        kpos = ws + lax.broadcasted_iota(jnp.int32, (tq, win_k), 1)
        valid = jnp.abs(qpos - kpos) <= WINDOW
        cos = cos_ref[pl.ds(r0, tq), :]
        sin = sin_ref[pl.ds(r0, tq), :]
        qs = [_rope(_head(ql_ref, g).astype(F32), cos, sin).astype(BF16) for g in range(GQA_GROUP)]
        scores = [(_dot_nt(q, kw), _dot_nt(q, kc_ref[...])) for q in qs]
        for g, (s_w, s_c) in enumerate(scores):
            s_w = jnp.where(valid, s_w * scale, NEG_BIG)
            o = _softmax_pv([(s_w, vw), (s_c * scale, vc_ref[...])], sink_ref[hk * GQA_GROUP + g] * LOG2E)
            _store_gated(ol_ref, zl_ref, g, o)


def _attn_b_kernel(*refs, with_ctx, n_lat, scale):
    if with_ctx:
        (qc_ref, zc_ref, ql_ref, zl_ref, kc_ref, vc_ref, kl_ref, vl_ref, cos_ref, sin_ref, qn_ref, kn_ref,
         oc_ref, ol_ref, kc_s, kl_s) = refs
    else:
        (ql_ref, zl_ref, kc_ref, vc_ref, kl_ref, vl_ref, cos_ref, sin_ref, qn_ref, kn_ref,
         ol_ref, kc_s, kl_s) = refs
    qi = pl.program_id(2)
    q0 = 1 if with_ctx else 0
    tq = ATTN_TQ

    @pl.when(qi == 0)
    def _prep():
        kn = kn_ref[...] * scale
        kc_s[...] = _rms(kc_ref[...].astype(F32), kn).astype(BF16)

        def body(i, carry):
            r = pl.multiple_of(i * tq, tq)
            k = _rms(kl_ref[pl.ds(r, tq), :].astype(F32), kn)
            kl_s[pl.ds(r, tq), :] = _rope(k, cos_ref[pl.ds(r, tq), :], sin_ref[pl.ds(r, tq), :]).astype(BF16)
            return carry
        lax.fori_loop(0, n_lat // tq, body, 0)

    if with_ctx:
        @pl.when(qi == 0)
        def _ctx():
            for g in range(GQA_GROUP):
                q = _rms(_head(qc_ref, g).astype(F32), qn_ref[...]).astype(BF16)
                o = _softmax_pv([(_dot_nt(q, kc_s[...]), vc_ref[...])], None)
                _store_gated(oc_ref, zc_ref, g, o)

    @pl.when(qi >= q0)
    def _lat():
        r0 = pl.multiple_of((qi - q0) * tq, tq)
        cos = cos_ref[pl.ds(r0, tq), :]
        sin = sin_ref[pl.ds(r0, tq), :]
        qs = [_rope(_rms(_head(ql_ref, g).astype(F32), qn_ref[...]), cos, sin).astype(BF16) for g in range(GQA_GROUP)]
        scores = [(_dot_nt(q, kc_s[...]), _dot_nt(q, kl_s[...])) for q in qs]
        for g, (s_c, s_l) in enumerate(scores):
            o = _softmax_pv([(s_c, vc_ref[...]), (s_l, vl_ref[...])], None)
            _store_gated(ol_ref, zl_ref, g, o)


def _attn_call(kind, p_c, p_l, extras, cos, sin, batch, with_ctx):
    m = p_c.shape[0] // batch
    n = p_l.shape[0] // batch
    tq = ATTN_TQ
    assert m == tq and n % tq == 0 and n >= tq + 2 * WINDOW
    nq = n // tq
    q0 = 1 if with_ctx else 0
    if kind == "a":
        qcol, kcol, vcol, zcol = A_Q, A_K, A_V, A_Z
    else:
        qcol, kcol, vcol, zcol = B_Q, B_K, B_V, B_Z
    gw, hd = GROUP_WIDTH, HEAD_DIM

    def lat_row(b, hk, qi):
        return b * nq + jnp.maximum(qi - q0, 0)

    def grp_c(col):
        return pl.BlockSpec((tq, gw), lambda b, hk, qi: (b, col // gw + hk))

    def grp_l(col):
        return pl.BlockSpec((tq, gw), lambda b, hk, qi: (lat_row(b, hk, qi), col // gw + hk))

    def kv(rows, col):
        return pl.BlockSpec((rows, hd), lambda b, hk, qi: (b, col // hd + hk))

    def full(a):
        return pl.BlockSpec(a.shape, lambda b, hk, qi: (0,) * a.ndim)

    in_specs, args = [], []
    if kind == "a":
        in_specs.append(pl.BlockSpec(memory_space=pltpu.SMEM))
        args.append(extras[0])
    if with_ctx:
        in_specs += [grp_c(qcol), grp_c(zcol)]
        args += [p_c, p_c]
    in_specs += [grp_l(qcol), grp_l(zcol), kv(m, kcol), kv(m, vcol), kv(n, kcol), kv(n, vcol), full(cos), full(sin)]
    args += [p_l, p_l, p_c, p_c, p_l, p_l, cos, sin]
    scratch = [pltpu.VMEM((n, hd), BF16)]
    if kind == "b":
        in_specs += [full(extras[0]), full(extras[1])]
        args += list(extras)
        scratch = [pltpu.VMEM((m, hd), BF16)] + scratch

    out_l_spec = pl.BlockSpec((tq, gw), lambda b, hk, qi: (lat_row(b, hk, qi), hk))
    out_l_shape = jax.ShapeDtypeStruct((batch * n, Q_WIDTH), BF16)
    if with_ctx:
        out_specs = [pl.BlockSpec((tq, gw), lambda b, hk, qi: (b, hk)), out_l_spec]
        out_shape = [jax.ShapeDtypeStruct((batch * m, Q_WIDTH), BF16), out_l_shape]
    else:
        out_specs, out_shape = [out_l_spec], [out_l_shape]

    body = _attn_a_kernel if kind == "a" else _attn_b_kernel
    outs = pl.pallas_call(
        functools.partial(body, with_ctx=with_ctx, n_lat=n, scale=HEAD_DIM ** -0.5 * LOG2E),
        grid=(batch, A_KV_HEADS, nq + q0),
        in_specs=in_specs,
        out_specs=out_specs,
        out_shape=out_shape,
        scratch_shapes=scratch,
        compiler_params=_params("parallel", "parallel", "arbitrary"),
        name="attn_" + kind,
    )(*args)
    return (outs[0], outs[1]) if with_ctx else (None, outs[0])


HGRN_PAIRS = ((0, 32), (1, 2), (4, 8), (16, None))
HGRN_SELECT_LEVELS = (1, 2, 4, 8)
HGRN_SIGN_LEVELS = (1, 2, 4)


def _hgrn_constants():
    c = HGRN_CHUNK
    t = np.arange(c)[:, None]
    s = np.arange(c)[None, :]
    tri = s <= t

    def pair_mask(h):
        if h is None:
            return np.zeros((c, c), bool)
        if h == 0:
            return t == s
        blk = 2 * h
        return (t // blk == s // blk) & (t % blk >= h) & (s % blk < h)

    mw = np.stack([np.stack([np.concatenate([pair_mask(a).T if rev else pair_mask(a),
                                             pair_mask(b).T if rev else pair_mask(b)], axis=1)
                             for a, b in HGRN_PAIRS]) for rev in (False, True)])
    rows = np.broadcast_to(np.arange(c)[:, None], (c, HEAD_DIM))

    def t_role(h, rev):
        ph = rows % (2 * h)
        return (ph < h) if rev else (ph >= h)

    role = np.stack([np.stack([t_role(h, rev) for h in HGRN_SELECT_LEVELS]) for rev in (False, True)])
    sgn = np.stack([np.stack([np.where(t_role(h, rev), 1.0, -1.0) for h in HGRN_SIGN_LEVELS])
                    for rev in (False, True)])
    return (jnp.asarray(np.stack([tri, tri.T]), dtype=BF16), jnp.asarray(mw, dtype=BF16),
            jnp.asarray(role, dtype=BF16), jnp.asarray(sgn, dtype=F32))


def _hgrn_cumulate(g, tri):
    g_hi = g.astype(BF16)
    g_lo = (g - g_hi.astype(F32)).astype(BF16)
    cc = _dot(tri, jnp.concatenate([g_hi, g_lo], axis=1))
    return (cc[:, :HEAD_DIM] + cc[:, HEAD_DIM:]) * LOG2E


def _hgrn_state_operands(q, kk, cum_ref, r, rev):
    c = HGRN_CHUNK
    cum = cum_ref[pl.ds(r, c), :]
    tot = cum_ref[pl.ds(r + (0 if rev else c - 1), 1), :]
    return q * jnp.exp2(cum).astype(BF16), kk * jnp.exp2(tot - cum).astype(BF16), tot


def _hgrn_pair_operands(q, kk, cum_ref, r, role_ref, sgn_ref, rev):
    c = HGRN_CHUNK
    cum = cum_ref[pl.ds(r, c), :]

    def row(k):
        return cum_ref[pl.ds(r + k, 1), :]

    sub = lax.broadcasted_iota(jnp.int32, (8, HEAD_DIM), 0)

    def operand(h):
        blk = 2 * h
        if h >= 8:
            d_parts, o_parts = [], []
            for b in range(c // blk):
                lo, mid, hi = b * blk, b * blk + h, (b + 1) * blk
                ref = row(mid if rev else mid - 1)
                if rev:
                    d_parts += [cum[lo:mid] - ref, ref - cum[mid:hi]]
                    o_parts += [q[lo:mid], kk[mid:hi]]
                else:
                    d_parts += [ref - cum[lo:mid], cum[mid:hi] - ref]
                    o_parts += [kk[lo:mid], q[mid:hi]]
            d = jnp.concatenate(d_parts, axis=0)
            if h >= 16:
                return jnp.concatenate(o_parts, axis=0) * jnp.exp2(d).astype(BF16)
        else:
            parts = []
            for grp in range(c // 8):
                base = grp * 8
                if h == 4:
                    ref = row(base + (4 if rev else 3))
                elif h == 2:
                    ref = jnp.where(sub < 4, row(base + (2 if rev else 1)), row(base + (6 if rev else 5)))
                else:
                    k0 = base + (1 if rev else 0)
                    ref = jnp.where(sub < 2, row(k0), jnp.where(sub < 4, row(k0 + 2),
                                                               jnp.where(sub < 6, row(k0 + 4), row(k0 + 6))))
                parts.append(cum[base:base + 8] - ref)
            d = jnp.concatenate(parts, axis=0) * sgn_ref[HGRN_SIGN_LEVELS.index(h)]
        opnd = jnp.where(role_ref[HGRN_SELECT_LEVELS.index(h)] > 0, q, kk)
        return opnd * jnp.exp2(d).astype(BF16)

    def sides(item):
        if item == 0:
            return q, kk
        z = operand(item)
        return z, z

    zeros = jnp.zeros((c, HEAD_DIM), BF16)
    pairs = []
    for a, b in HGRN_PAIRS:
        la, ra = sides(a)
        if b is None:
            pairs.append((la, jnp.concatenate([ra, zeros], axis=0)))
        else:
            lb, rb = sides(b)
            pairs.append((jnp.concatenate([la, lb], axis=1),
                          jnp.concatenate([jnp.concatenate([ra, zeros], axis=1),
                                           jnp.concatenate([zeros, rb], axis=1)], axis=0)))
    return pairs


def _hgrn_blocks(x):
    return x[:HGRN_CHUNK, :HEAD_DIM], x[HGRN_CHUNK:, HEAD_DIM:]


def _hgrn_intra_products(pairs_f, pairs_b):
    out = [_hgrn_blocks(_dot_nt(jnp.concatenate([pf[0], pb[0]], axis=0), jnp.concatenate([pf[1], pb[1]], axis=0)))
           for pf, pb in zip(pairs_f, pairs_b)]
    return [o[0] for o in out], [o[1] for o in out]


def _hgrn_intra_attn(products, mw_ref, d):
    acc = None
    for j, t in enumerate(products):
        t = t.astype(BF16) * mw_ref[d, j]
        acc = t if acc is None else acc + t
    return acc


def _hgrn_outputs(fwd, bwd):
    (qf, kf, tf), af, vf, sf = fwd
    (qb, kb, tb), ab, vb, sb = bwd
    st_f, st_b = sf[...], sb[...]
    o_f, o_b = _hgrn_blocks(_dot_nt(jnp.concatenate([qf, qb], axis=0),
                                    jnp.concatenate([st_f.astype(BF16), st_b.astype(BF16)], axis=0)))
    sf[...] = st_f * jnp.exp2(tf) + _dot_tn(vf, kf)
    sb[...] = st_b * jnp.exp2(tb) + _dot_tn(vb, kb)
    i_f, i_b = _hgrn_blocks(_dot(jnp.concatenate([af, ab], axis=0),
                                 jnp.concatenate([jnp.concatenate([vf, vf], axis=0),
                                                  jnp.concatenate([vb, vb], axis=0)], axis=1)))
    return o_f + i_f, o_b + i_b


def _hgrn_kernel(*refs, with_ctx, m_ctx, n_lat):
    (qc_ref, kfc_ref, kbc_ref, ic_ref, gfc_ref, gbc_ref,
     ql_ref, kfl_ref, kbl_ref, il_ref, gfl_ref, gbl_ref) = refs[:12]
    refs = refs[12:]
    if with_ctx:
        zc_ref, refs = refs[0], refs[1:]
    zl_ref, gain_ref, tri_ref, mw_ref, role_ref, sgn_ref = refs[:6]
    refs = refs[6:]
    if with_ctx:
        oc_ref, refs = refs[0], refs[1:]
    ol_ref = refs[0]
    st_f, st_b, q_s, v_s, kkf_s, kkb_s, cumf_s, cumb_s, of_s, ob_s = refs[1:]
    chunk = HGRN_CHUNK
    nc_c, nc_l = m_ctx // chunk, n_lat // chunk

    def gather(q_ref, kf_ref, kb_ref, i_ref, gf_ref, gb_ref, n_chunks, base):
        def body(c, carry):
            r = pl.multiple_of(c * chunk, chunk)
            src = pl.ds(r, chunk)
            dst = pl.ds(pl.multiple_of(base + r, chunk), chunk)
            q_s[dst, :] = q_ref[src, :]
            v_s[dst, :] = i_ref[src, :]
            kkf_s[dst, :] = kf_ref[src, :]
            kkb_s[dst, :] = kb_ref[src, :]
            cumf_s[dst, :] = _hgrn_cumulate(gf_ref[src, :], tri_ref[0])
            cumb_s[dst, :] = _hgrn_cumulate(gb_ref[src, :], tri_ref[1])
            return carry
        lax.fori_loop(0, n_chunks, body, 0, unroll=8 if n_chunks % 8 == 0 else 4)

    gather(qc_ref, kfc_ref, kbc_ref, ic_ref, gfc_ref, gbc_ref, nc_c, 0)
    gather(ql_ref, kfl_ref, kbl_ref, il_ref, gfl_ref, gbl_ref, nc_l, m_ctx)

    st_f[...] = jnp.zeros_like(st_f)
    st_b[...] = jnp.zeros_like(st_b)
    n_chunks = nc_c + nc_l

    def rows(c):
        cb = jnp.where(c < nc_c, nc_c - 1 - c, n_chunks - 1 - (c - nc_c))
        return pl.multiple_of(c * chunk, chunk), pl.multiple_of(cb * chunk, chunk)

    def products(c):
        rf, rb = rows(c)
        return _hgrn_intra_products(
            _hgrn_pair_operands(q_s[pl.ds(rf, chunk), :], kkf_s[pl.ds(rf, chunk), :], cumf_s, rf,
                                role_ref.at[0], sgn_ref.at[0], False),
            _hgrn_pair_operands(q_s[pl.ds(rb, chunk), :], kkb_s[pl.ds(rb, chunk), :], cumb_s, rb,
                                role_ref.at[1], sgn_ref.at[1], True))

    def attn(prods):
        return _hgrn_intra_attn(prods[0], mw_ref, 0), _hgrn_intra_attn(prods[1], mw_ref, 1)

    def body(c, carry):
        attn_f, attn_b = carry
        nxt = products(jnp.minimum(c + 1, n_chunks - 1))
        rf, rb = rows(c)
        of_s[pl.ds(rf, chunk), :], ob_s[pl.ds(rb, chunk), :] = _hgrn_outputs(
            (_hgrn_state_operands(q_s[pl.ds(rf, chunk), :], kkf_s[pl.ds(rf, chunk), :], cumf_s, rf, False),
             attn_f, v_s[pl.ds(rf, chunk), :], st_f),
            (_hgrn_state_operands(q_s[pl.ds(rb, chunk), :], kkb_s[pl.ds(rb, chunk), :], cumb_s, rb, True),
             attn_b, v_s[pl.ds(rb, chunk), :], st_b))
        return attn(nxt)
    lax.fori_loop(0, n_chunks, body, attn(products(0)),
                  unroll=HGRN_UNROLL if n_chunks % HGRN_UNROLL == 0 else 1)

    def finish(z_ref, o_ref, rows, base):
        blk = HGRN_OUT_ROWS

        def body(i, carry):
            r = pl.multiple_of(i * blk, blk)
            src = pl.ds(pl.multiple_of(base + r, blk), blk)
            o = _rms((of_s[src, :] + ob_s[src, :]) * (HEAD_DIM ** -0.5), gain_ref[...])
            o_ref[pl.ds(r, blk), :] = (o * _silu(z_ref[pl.ds(r, blk), :].astype(F32))).astype(BF16)
            return carry
        lax.fori_loop(0, rows // blk, body, 0, unroll=2 if (rows // blk) % 2 == 0 else 1)

    if with_ctx:
        finish(zc_ref, oc_ref, m_ctx, 0)
    finish(zl_ref, ol_ref, n_lat, m_ctx)


def _hgrn_call(p_c, lf_c, p_l, lf_l, gain, batch, with_ctx):
    m = p_c.shape[0] // batch
    n = p_l.shape[0] // batch
    hd = HEAD_DIM
    assert m % 256 == 0 and n % 256 == 0
    consts = _hgrn_constants()

    def col(rows, c):
        return pl.BlockSpec((rows, hd), lambda b, h: (b, c // hd + h))

    def full(a):
        return pl.BlockSpec(a.shape, lambda b, h: (0,) * a.ndim)

    def side(rows, p, lf):
        return ([col(rows, C_Q), col(rows, C_ZF), col(rows, C_ZB), col(rows, C_I), col(rows, 0), col(rows, C_HEADS * hd)],
                [p, p, p, p, lf, lf])

    specs_c, args_c = side(m, p_c, lf_c)
    specs_l, args_l = side(n, p_l, lf_l)
    in_specs, args = specs_c + specs_l, args_c + args_l
    if with_ctx:
        in_specs.append(col(m, C_Z))
        args.append(p_c)
    in_specs += [col(n, C_Z), full(gain)] + [full(a) for a in consts]
    args += [p_l, gain, *consts]

    out_l_spec = pl.BlockSpec((n, hd), lambda b, h: (b, h))
    out_l_shape = jax.ShapeDtypeStruct((batch * n, Q_WIDTH), BF16)
    if with_ctx:
        out_specs = [pl.BlockSpec((m, hd), lambda b, h: (b, h)), out_l_spec]
        out_shape = [jax.ShapeDtypeStruct((batch * m, Q_WIDTH), BF16), out_l_shape]
    else:
        out_specs, out_shape = [out_l_spec], [out_l_shape]

    outs = pl.pallas_call(
        functools.partial(_hgrn_kernel, with_ctx=with_ctx, m_ctx=m, n_lat=n),
        grid=(batch, C_HEADS),
        in_specs=in_specs,
        out_specs=out_specs,
        out_shape=out_shape,
        scratch_shapes=([pltpu.VMEM((hd, hd), F32)] * 2 + [pltpu.VMEM((m + n, hd), BF16)] * 4
                        + [pltpu.VMEM((m + n, hd), F32)] * 4),
        compiler_params=_params("parallel", "parallel"),
        name="hgrn2",
    )(*args)
    return (outs[0], outs[1]) if with_ctx else (None, outs[0])


def _merge_kernel(*refs, tail):
    (x_ref, ga_ref, gb_ref, gc_ref, ka_ref, kb_ref, kc_ref, gt_ref, wa_ref, wb_ref, wc_ref, wo_ref) = refs[:12]
    mix = jax.nn.sigmoid(ka_ref[...].astype(F32)) * _dot(ga_ref[...], wa_ref[...])
    mix = mix + jax.nn.sigmoid(kb_ref[...].astype(F32)) * _dot(gb_ref[...], wb_ref[...])
    mix = mix + jax.nn.sigmoid(kc_ref[...].astype(F32)) * _dot(gc_ref[...], wc_ref[...])
    new = x_ref[...] + gt_ref[...] * _dot(mix.astype(BF16), wo_ref[...])
    if tail == "final":
        refs[13][...] = _rms(new, refs[12][...])
    else:
        g_ref, sc_ref, sh_ref, o_ref, h_ref = refs[12:]
        o_ref[...] = new
        h_ref[...] = _modulated_norm(new, g_ref[...], sc_ref[...], sh_ref[...]).astype(BF16)


def _merge_call(x, ga, gb, gc, p, mod, wa, wb, wc, wo, rows_per_mod, mod_row0, final_g=None, next_norm=None):
    rows, d = x.shape
    tm = _row_tile(rows, rows_per_mod, MERGE_TM)

    def rowblk(width, colblk=0):
        return pl.BlockSpec((tm, width), lambda i: (i, colblk))

    def weight(w):
        return pl.BlockSpec(w.shape, lambda i: (0, 0), pipeline_mode=pl.Buffered(1))

    def mod_spec(which):
        return pl.BlockSpec((None, None, 1, d), lambda i: (mod_row0 + (i * tm) // rows_per_mod, which, 0, 0))

    vec = pl.BlockSpec((1, d), lambda i: (0, 0))
    in_specs = [rowblk(d), rowblk(Q_WIDTH), rowblk(Q_WIDTH), rowblk(Q_WIDTH),
                rowblk(d, GATES // d), rowblk(d, GATES // d + 1), rowblk(d, GATES // d + 2),
                mod_spec(2), weight(wa), weight(wb), weight(wc), weight(wo)]
    args = [x, ga, gb, gc, p, p, p, mod, wa, wb, wc, wo]
    if final_g is not None:
        in_specs.append(vec)
        args.append(final_g.reshape(1, d))
        out_specs, out_shape = rowblk(d), jax.ShapeDtypeStruct((rows, d), F32)
    else:
        next_g, next_mod = next_norm
        in_specs += [vec, mod_spec(1), mod_spec(0)]
        args += [next_g.reshape(1, d), next_mod, next_mod]
        out_specs = [rowblk(d), rowblk(d)]
        out_shape = [jax.ShapeDtypeStruct((rows, d), F32), jax.ShapeDtypeStruct((rows, d), BF16)]
    return pl.pallas_call(
        functools.partial(_merge_kernel, tail="final" if final_g is not None else "next"),
        grid=(rows // tm,),
        in_specs=in_specs,
        out_specs=out_specs,
        out_shape=out_shape,
        compiler_params=_params("parallel"),
        name="merge_out",
    )(*args)


def _rope_tables(n):
    rows = n // GRID_W
    row = jnp.repeat(jnp.arange(rows, dtype=F32), GRID_W)
    col = jnp.tile(jnp.arange(GRID_W, dtype=F32), rows)
    n_freq = HEAD_DIM // 4
    inv_freq = ROPE_THETA ** (-jnp.arange(n_freq, dtype=F32) / n_freq)
    ang = jnp.concatenate([row[:, None] * inv_freq, col[:, None] * inv_freq], axis=-1)
    ang = jnp.concatenate([ang, ang], axis=-1)
    sign = jnp.where(jnp.arange(HEAD_DIM) < HEAD_DIM // 2, -1.0, 1.0).astype(F32)
    return jnp.cos(ang), jnp.sin(ang) * sign


def kernel(x, c, ctx, c_ctx, ada_w, ada_b, norm_g, w_in, a_sink, b_q_norm, b_k_norm, c_lower_bound, c_out_norm,
           w_branch_a, w_branch_b, w_branch_c, w_out, final_norm_g):
    batch, n, d = x.shape
    m = ctx.shape[1]
    depth = ada_w.shape[0]
    assert batch + 1 <= MOD_ROWS
    cos, sin = _rope_tables(n)
    lb_all = jnp.cumsum(jax.nn.softmax(c_lower_bound.astype(F32), axis=0), axis=0)
    lb_all = lb_all - lb_all[0:1]

    cv = jnp.zeros((MOD_ROWS, d), F32).at[:batch].set(c).at[batch].set(c_ctx)
    mod = _ada_call(cv, ada_w, ada_b).reshape(depth, MOD_ROWS, 3, 1, d)

    x_l = x.reshape(batch * n, d)
    x_c = ctx.reshape(batch * m, d)
    h_c = h_l = None
    for l in range(depth):
        with_ctx = l < depth - 1
        norm = (norm_g[l], mod[l]) if l == 0 else None
        p_c, lf_c = _inproj_call(x_c if l == 0 else h_c, norm, w_in, l, lb_all[l], batch * m, batch,
                                 col_tiles=None if with_ctx else LAST_LAYER_CTX_TILES)
        p_l, lf_l = _inproj_call(x_l if l == 0 else h_l, norm, w_in, l, lb_all[l], n, 0)
        ga_c, ga_l = _attn_call("a", p_c, p_l, (a_sink[l],), cos, sin, batch, with_ctx)
        gb_c, gb_l = _attn_call("b", p_c, p_l, (b_q_norm[l].reshape(1, -1), b_k_norm[l].reshape(1, -1)),
                                cos, sin, batch, with_ctx)
        gc_c, gc_l = _hgrn_call(p_c, lf_c, p_l, lf_l, c_out_norm[l].reshape(1, -1), batch, with_ctx)
        ws = (w_branch_a[l].astype(BF16), w_branch_b[l].astype(BF16), w_branch_c[l].astype(BF16),
              w_out[l].astype(BF16))
        if with_ctx:
            nxt = (norm_g[l + 1], mod[l + 1])
            x_c, h_c = _merge_call(x_c, ga_c, gb_c, gc_c, p_c, mod[l], *ws, batch * m, batch, next_norm=nxt)
            x_l, h_l = _merge_call(x_l, ga_l, gb_l, gc_l, p_l, mod[l], *ws, n, 0, next_norm=nxt)
        else:
            x_l = _merge_call(x_l, ga_l, gb_l, gc_l, p_l, mod[l], *ws, n, 0, final_g=final_norm_g)
    return x_l.reshape(batch, n, d)
```

```python
import functools

import numpy as np
import jax
import jax.numpy as jnp
from jax import lax
from jax.experimental import pallas as pl
from jax.experimental.pallas import tpu as pltpu

F32 = jnp.float32
BF16 = jnp.bfloat16

HEAD_DIM = 128
A_HEADS, A_KV_HEADS = 8, 2
B_HEADS, B_KV_HEADS = 8, 2
C_HEADS = 8
WINDOW = 128
GRID_W = 64
ROPE_THETA = 10000.0
NORM_EPS = 1e-6
N_BRANCH = 3
GQA_GROUP = A_HEADS // A_KV_HEADS
Q_WIDTH = A_HEADS * HEAD_DIM
GROUP_WIDTH = GQA_GROUP * HEAD_DIM

A_Q, A_K, A_V, A_Z = 0, 1024, 1280, 1536
B_Q, B_K, B_V, B_Z = 2560, 3584, 3840, 4096
C_Q, C_ZF, C_ZB, C_I, C_Z = 5120, 6144, 7168, 8192, 9216
GATES = 10240
IN_PROJ_TM = 1024
IN_PROJ_TN = 1024
GATE_SLAB = 256
MERGE_TM = 256
ADA_TN = 512
HGRN_OUT_ROWS = 256
LAST_LAYER_CTX_TILES = tuple(sorted({c // IN_PROJ_TN for c in (A_K, A_V, B_K, B_V, C_Q, C_ZF, C_ZB, C_I)}))

ATTN_TQ = 256
HGRN_CHUNK = 64
HGRN_UNROLL = 9
MOD_ROWS = 16
LOG2E = 1.4426950408889634
NEG_BIG = -1e30
VMEM_LIMIT = 56 * 1024 * 1024


def _dot(a, b):
    return jnp.dot(a, b, preferred_element_type=F32)


def _dot_nt(a, b):
    return lax.dot_general(a, b, (((1,), (1,)), ((), ())), preferred_element_type=F32)


def _dot_tn(a, b):
    return lax.dot_general(a, b, (((0,), (0,)), ((), ())), preferred_element_type=F32)


def _rms(x, g):
    return x * lax.rsqrt(jnp.mean(x * x, axis=-1, keepdims=True) + NORM_EPS) * g


def _rope(x, cos, sin_signed):
    return x * cos + pltpu.roll(x, HEAD_DIM // 2, 1) * sin_signed


def _silu(x):
    return x * jax.nn.sigmoid(x)


def _params(*sem, flags=None):
    return pltpu.CompilerParams(dimension_semantics=sem, vmem_limit_bytes=VMEM_LIMIT, flags=flags)


def _ada_kernel(cv_ref, w_ref, b_ref, o_ref):
    s = _silu(cv_ref[...]).astype(BF16)
    o_ref[...] = _dot(s, w_ref[...].astype(BF16)) + b_ref[...]


def _ada_call(cv, ada_w, ada_b, tn=ADA_TN):
    depth, d, n3 = ada_w.shape
    return pl.pallas_call(
        _ada_kernel,
        grid=(depth, n3 // tn),
        in_specs=[
            pl.BlockSpec((MOD_ROWS, d), lambda l, j: (0, 0)),
            pl.BlockSpec((None, d, tn), lambda l, j: (l, 0, j)),
            pl.BlockSpec((None, 1, tn), lambda l, j: (l, 0, j)),
        ],
        out_specs=pl.BlockSpec((None, MOD_ROWS, tn), lambda l, j: (l, 0, j)),
        out_shape=jax.ShapeDtypeStruct((depth, MOD_ROWS, n3), F32),
        compiler_params=_params("parallel", "parallel"),
        name="ada_mod",
    )(cv, ada_w, ada_b.reshape(depth, 1, n3))


def _modulated_norm(x, g, sc, sh):
    return _rms(x, g) * (1.0 + sc) + sh


def _forget_gate(z, lb):
    log_lb, log1m_lb, one_m_lb = jnp.log(lb), jnp.log(1.0 - lb), 1.0 - lb
    u = jnp.exp(-jnp.abs(z))
    log_sig = jnp.minimum(z, 0.0) - jnp.log(1.0 + u)
    b = log1m_lb + log_sig
    g = jnp.maximum(log_lb, b) + jnp.log(1.0 + jnp.exp(-jnp.abs(log_lb - b)))
    kk = one_m_lb * jnp.where(z >= 0.0, u, 1.0) * (1.0 / (1.0 + u))
    return g, kk


def _inproj_kernel(*refs, pre_normed, tiles):
    if pre_normed:
        h_ref, w_ref, lb_ref, o_ref, lf_ref = refs
    else:
        x_ref, g_ref, sc_ref, sh_ref, w_ref, lb_ref, o_ref, lf_ref, h_ref = refs

        @pl.when(pl.program_id(1) == 0)
        def _():
            h_ref[...] = _modulated_norm(x_ref[...], g_ref[...], sc_ref[...], sh_ref[...]).astype(BF16)

    tile = pl.program_id(1) if tiles is None else _lookup(tiles, pl.program_id(1))
    is_gate = (tile == C_ZF // IN_PROJ_TN) | (tile == C_ZB // IN_PROJ_TN)

    @pl.when(jnp.logical_not(is_gate))
    def _():
        o_ref[...] = _dot(h_ref[...], w_ref[...].astype(BF16)).astype(BF16)

    @pl.when(is_gate)
    def _():
        acc = _dot(h_ref[...], w_ref[...].astype(BF16))
        for c0 in range(0, IN_PROJ_TN, GATE_SLAB):
            g, kk = _forget_gate(acc[:, c0:c0 + GATE_SLAB], lb_ref[:, c0:c0 + GATE_SLAB])
            o_ref[:, c0:c0 + GATE_SLAB] = kk.astype(BF16)
            lf_ref[:, c0:c0 + GATE_SLAB] = g


def _lookup(table, j):
    v = table[-1]
    for k in range(len(table) - 2, -1, -1):
        v = jnp.where(j == k, table[k], v)
    return v


def _row_tile(rows, rows_per_mod, cap):
    t = cap
    while rows % t or rows_per_mod % t:
        t //= 2
    return t


def _inproj_call(x, norm, w_all, layer, lb, rows_per_mod, mod_row0, col_tiles=None):
    rows, d = x.shape
    n = w_all.shape[2]
    tn = IN_PROJ_TN
    tm = _row_tile(rows, rows_per_mod, IN_PROJ_TM)
    tiles = tuple(range(n // tn)) if col_tiles is None else tuple(col_tiles)
    zb_tile = C_ZB // tn

    def col(j):
        return j if col_tiles is None else _lookup(tiles, j)

    def dirn(j):
        return jnp.where(col(j) >= zb_tile, 1, 0)

    in_specs = [pl.BlockSpec((tm, d), lambda i, j: (i, 0))]
    args = [x]
    scratch = []
    if norm is not None:
        g, mod = norm

        def mod_spec(which):
            return pl.BlockSpec((None, None, 1, d),
                                lambda i, j: (mod_row0 + (i * tm) // rows_per_mod, which, 0, 0))

        in_specs += [pl.BlockSpec((1, d), lambda i, j: (0, 0)), mod_spec(1), mod_spec(0)]
        args += [g.reshape(1, d), mod, mod]
        scratch = scratch + [pltpu.VMEM((tm, d), BF16)]
    in_specs += [pl.BlockSpec((None, d, tn), lambda i, j: (layer, 0, col(j))),
                 pl.BlockSpec((None, 1, tn), lambda i, j: (dirn(j), 0, 0))]
    args += [w_all, lb.reshape(2, 1, tn)]
    return pl.pallas_call(
        functools.partial(_inproj_kernel, pre_normed=norm is None, tiles=None if col_tiles is None else tiles),
        grid=(rows // tm, len(tiles)),
        in_specs=in_specs,
        out_specs=[pl.BlockSpec((tm, tn), lambda i, j: (i, col(j))),
                   pl.BlockSpec((tm, tn), lambda i, j: (i, dirn(j)))],
        out_shape=[jax.ShapeDtypeStruct((rows, n), BF16), jax.ShapeDtypeStruct((rows, 2 * tn), F32)],
        scratch_shapes=scratch,
        compiler_params=_params("parallel", "arbitrary"),
        name="in_proj",
    )(*args)


def _softmax_pv(parts, extra_logit):
    m = functools.reduce(jnp.maximum, [jnp.max(s, axis=-1, keepdims=True) for s, _ in parts])
    if extra_logit is not None:
        m = jnp.maximum(m, extra_logit)
    ps = [jnp.exp2(s - m) for s, _ in parts]
    l = functools.reduce(jnp.add, [jnp.sum(p, axis=-1, keepdims=True) for p in ps])
    if extra_logit is not None:
        l = l + jnp.exp2(extra_logit - m)
    o = functools.reduce(jnp.add, [_dot(p.astype(BF16), v) for p, (_, v) in zip(ps, parts)])
    return o * (1.0 / l)


def _head(ref, g):
    return ref[:, g * HEAD_DIM:(g + 1) * HEAD_DIM]


def _store_gated(o_ref, z_ref, g, o):
    z = _head(z_ref, g).astype(F32)
    o_ref[:, g * HEAD_DIM:(g + 1) * HEAD_DIM] = (o * _silu(z)).astype(BF16)


def _attn_a_kernel(*refs, with_ctx, n_lat, scale):
    if with_ctx:
        (sink_ref, qc_ref, zc_ref, ql_ref, zl_ref, kc_ref, vc_ref, kl_ref, vl_ref, cos_ref, sin_ref,
         oc_ref, ol_ref, kl_s) = refs
    else:
        (sink_ref, ql_ref, zl_ref, kc_ref, vc_ref, kl_ref, vl_ref, cos_ref, sin_ref, ol_ref, kl_s) = refs
    hk = pl.program_id(1)
    qi = pl.program_id(2)
    q0 = 1 if with_ctx else 0
    tq = ATTN_TQ

    @pl.when(qi == 0)
    def _prep():
        def body(i, carry):
            r = pl.multiple_of(i * tq, tq)
            k = kl_ref[pl.ds(r, tq), :].astype(F32)
            kl_s[pl.ds(r, tq), :] = _rope(k, cos_ref[pl.ds(r, tq), :], sin_ref[pl.ds(r, tq), :]).astype(BF16)
            return carry
        lax.fori_loop(0, n_lat // tq, body, 0)

    if with_ctx:
        @pl.when(qi == 0)
        def _ctx():
            for g in range(GQA_GROUP):
                s = _dot_nt(_head(qc_ref, g), kc_ref[...]) * scale
                o = _softmax_pv([(s, vc_ref[...])], sink_ref[hk * GQA_GROUP + g] * LOG2E)
                _store_gated(oc_ref, zc_ref, g, o)

    @pl.when(qi >= q0)
    def _lat():
        sb = WINDOW
        sub_k = sb + 2 * WINDOW
        for hh in range(tq // sb):
            rows = slice(hh * sb, (hh + 1) * sb)
            rs = pl.multiple_of((qi - q0) * tq + hh * sb, sb)
            ws = pl.multiple_of(jnp.clip(rs - WINDOW, 0, n_lat - sub_k), WINDOW)
            kw = kl_s[pl.ds(ws, sub_k), :]
            vw = vl_ref[pl.ds(ws, sub_k), :]
            qpos = rs + lax.broadcasted_iota(jnp.int32, (sb, sub_k), 0)
            kpos = ws + lax.broadcasted_iota(jnp.int32, (sb, sub_k), 1)
            valid = jnp.abs(qpos - kpos) <= WINDOW
            cos = cos_ref[pl.ds(rs, sb), :]
            sin = sin_ref[pl.ds(rs, sb), :]
            qs = [_rope(ql_ref[rows, g * HEAD_DIM:(g + 1) * HEAD_DIM].astype(F32), cos, sin).astype(BF16)
                  for g in range(GQA_GROUP)]
            scores = [(_dot_nt(q, kw), _dot_nt(q, kc_ref[...])) for q in qs]
            for g, (s_w, s_c) in enumerate(scores):
                cols = slice(g * HEAD_DIM, (g + 1) * HEAD_DIM)
                s_w = jnp.where(valid, s_w * scale, NEG_BIG)
                o = _softmax_pv([(s_w, vw), (s_c * scale, vc_ref[...])], sink_ref[hk * GQA_GROUP + g] * LOG2E)
                ol_ref[rows, cols] = (o * _silu(zl_ref[rows, cols].astype(F32))).astype(BF16)


def _attn_b_kernel(*refs, with_ctx, n_lat, scale):
    if with_ctx:
        (qc_ref, zc_ref, ql_ref, zl_ref, kc_ref, vc_ref, kl_ref, vl_ref, cos_ref, sin_ref, qn_ref, kn_ref,
         oc_ref, ol_ref, kc_s, kl_s) = refs
    else:
        (ql_ref, zl_ref, kc_ref, vc_ref, kl_ref, vl_ref, cos_ref, sin_ref, qn_ref, kn_ref,
         ol_ref, kc_s, kl_s) = refs
    qi = pl.program_id(2)
    q0 = 1 if with_ctx else 0
    tq = ATTN_TQ

    @pl.when(qi == 0)
    def _prep():
        kn = kn_ref[...] * scale
        kc_s[...] = _rms(kc_ref[...].astype(F32), kn).astype(BF16)

        def body(i, carry):
            r = pl.multiple_of(i * tq, tq)
            k = _rms(kl_ref[pl.ds(r, tq), :].astype(F32), kn)
            kl_s[pl.ds(r, tq), :] = _rope(k, cos_ref[pl.ds(r, tq), :], sin_ref[pl.ds(r, tq), :]).astype(BF16)
            return carry
        lax.fori_loop(0, n_lat // tq, body, 0)

    if with_ctx:
        @pl.when(qi == 0)
        def _ctx():
            for g in range(GQA_GROUP):
                q = _rms(_head(qc_ref, g).astype(F32), qn_ref[...]).astype(BF16)
                o = _softmax_pv([(_dot_nt(q, kc_s[...]), vc_ref[...])], None)
                _store_gated(oc_ref, zc_ref, g, o)

    @pl.when(qi >= q0)
    def _lat():
        r0 = pl.multiple_of((qi - q0) * tq, tq)
        cos = cos_ref[pl.ds(r0, tq), :]
        sin = sin_ref[pl.ds(r0, tq), :]
        qs = [_rope(_rms(_head(ql_ref, g).astype(F32), qn_ref[...]), cos, sin).astype(BF16) for g in range(GQA_GROUP)]
        scores = [(_dot_nt(q, kc_s[...]), _dot_nt(q, kl_s[...])) for q in qs]
        for g, (s_c, s_l) in enumerate(scores):
            o = _softmax_pv([(s_c, vc_ref[...]), (s_l, vl_ref[...])], None)
            _store_gated(ol_ref, zl_ref, g, o)


def _attn_call(kind, p_c, p_l, extras, cos, sin, batch, with_ctx):
    m = p_c.shape[0] // batch
    n = p_l.shape[0] // batch
    tq = ATTN_TQ
    assert m == tq and n % tq == 0 and n >= tq + 2 * WINDOW
    nq = n // tq
    q0 = 1 if with_ctx else 0
    if kind == "a":
        qcol, kcol, vcol, zcol = A_Q, A_K, A_V, A_Z
    else:
        qcol, kcol, vcol, zcol = B_Q, B_K, B_V, B_Z
    gw, hd = GROUP_WIDTH, HEAD_DIM

    def lat_row(b, hk, qi):
        return b * nq + jnp.maximum(qi - q0, 0)

    def grp_c(col):
        return pl.BlockSpec((tq, gw), lambda b, hk, qi: (b, col // gw + hk))

    def grp_l(col):
        return pl.BlockSpec((tq, gw), lambda b, hk, qi: (lat_row(b, hk, qi), col // gw + hk))

    def kv(rows, col):
        return pl.BlockSpec((rows, hd), lambda b, hk, qi: (b, col // hd + hk))

    def full(a):
        return pl.BlockSpec(a.shape, lambda b, hk, qi: (0,) * a.ndim)

    in_specs, args = [], []
    if kind == "a":
        in_specs.append(pl.BlockSpec(memory_space=pltpu.SMEM))
        args.append(extras[0])
    if with_ctx:
        in_specs += [grp_c(qcol), grp_c(zcol)]
        args += [p_c, p_c]
    in_specs += [grp_l(qcol), grp_l(zcol), kv(m, kcol), kv(m, vcol), kv(n, kcol), kv(n, vcol), full(cos), full(sin)]
    args += [p_l, p_l, p_c, p_c, p_l, p_l, cos, sin]
    scratch = [pltpu.VMEM((n, hd), BF16)]
    if kind == "b":
        in_specs += [full(extras[0]), full(extras[1])]
        args += list(extras)
        scratch = [pltpu.VMEM((m, hd), BF16)] + scratch

    out_l_spec = pl.BlockSpec((tq, gw), lambda b, hk, qi: (lat_row(b, hk, qi), hk))
    out_l_shape = jax.ShapeDtypeStruct((batch * n, Q_WIDTH), BF16)
    if with_ctx:
        out_specs = [pl.BlockSpec((tq, gw), lambda b, hk, qi: (b, hk)), out_l_spec]
        out_shape = [jax.ShapeDtypeStruct((batch * m, Q_WIDTH), BF16), out_l_shape]
    else:
        out_specs, out_shape = [out_l_spec], [out_l_shape]

    body = _attn_a_kernel if kind == "a" else _attn_b_kernel
    outs = pl.pallas_call(
        functools.partial(body, with_ctx=with_ctx, n_lat=n, scale=HEAD_DIM ** -0.5 * LOG2E),
        grid=(batch, A_KV_HEADS, nq + q0),
        in_specs=in_specs,
        out_specs=out_specs,
        out_shape=out_shape,
        scratch_shapes=scratch,
        compiler_params=_params("parallel", "parallel", "arbitrary"),
        name="attn_" + kind,
    )(*args)
    return (outs[0], outs[1]) if with_ctx else (None, outs[0])


HGRN_PAIRS = ((0, 32), (1, 2), (4, 8), (16, None))
HGRN_SELECT_LEVELS = (1, 2, 4, 8)
HGRN_SIGN_LEVELS = (1, 2, 4)


def _hgrn_constants():
    c = HGRN_CHUNK
    t = np.arange(c)[:, None]
    s = np.arange(c)[None, :]
    tri = s <= t

    def pair_mask(h):
        if h is None:
            return np.zeros((c, c), bool)
        if h == 0:
            return t == s
        blk = 2 * h
        return (t // blk == s // blk) & (t % blk >= h) & (s % blk < h)

    mw = np.stack([np.stack([np.concatenate([pair_mask(a).T if rev else pair_mask(a),
                                             pair_mask(b).T if rev else pair_mask(b)], axis=1)
                             for a, b in HGRN_PAIRS]) for rev in (False, True)])
    rows = np.broadcast_to(np.arange(c)[:, None], (c, HEAD_DIM))

    def t_role(h, rev):
        ph = rows % (2 * h)
        return (ph < h) if rev else (ph >= h)

    role = np.stack([np.stack([t_role(h, rev) for h in HGRN_SELECT_LEVELS]) for rev in (False, True)])
    sgn = np.stack([np.stack([np.where(t_role(h, rev), 1.0, -1.0) for h in HGRN_SIGN_LEVELS])
                    for rev in (False, True)])
    return (jnp.asarray(np.stack([tri, tri.T]), dtype=BF16), jnp.asarray(mw, dtype=BF16),
            jnp.asarray(role, dtype=BF16), jnp.asarray(sgn, dtype=F32))


def _hgrn_cumulate(g, tri):
    g_hi = g.astype(BF16)
    g_lo = (g - g_hi.astype(F32)).astype(BF16)
    cc = _dot(tri, jnp.concatenate([g_hi, g_lo], axis=1))
    return (cc[:, :HEAD_DIM] + cc[:, HEAD_DIM:]) * LOG2E


def _hgrn_state_operands(q, kk, cum_ref, r, rev):
    c = HGRN_CHUNK
    cum = cum_ref[pl.ds(r, c), :]
    tot = cum_ref[pl.ds(r + (0 if rev else c - 1), 1), :]
    return q * jnp.exp2(cum).astype(BF16), kk * jnp.exp2(tot - cum).astype(BF16), tot


def _hgrn_pair_operands(q, kk, cum_ref, r, role_ref, sgn_ref, rev):
    c = HGRN_CHUNK
    cum = cum_ref[pl.ds(r, c), :]

    def row(k):
        return cum_ref[pl.ds(r + k, 1), :]

    sub = lax.broadcasted_iota(jnp.int32, (8, HEAD_DIM), 0)

    def operand(h):
        blk = 2 * h
        if h >= 8:
            d_parts, o_parts = [], []
            for b in range(c // blk):
                lo, mid, hi = b * blk, b * blk + h, (b + 1) * blk
                ref = row(mid if rev else mid - 1)
                if rev:
                    d_parts += [cum[lo:mid] - ref, ref - cum[mid:hi]]
                    o_parts += [q[lo:mid], kk[mid:hi]]
                else:
                    d_parts += [ref - cum[lo:mid], cum[mid:hi] - ref]
                    o_parts += [kk[lo:mid], q[mid:hi]]
            d = jnp.concatenate(d_parts, axis=0)
            if h >= 16:
                return jnp.concatenate(o_parts, axis=0) * jnp.exp2(d).astype(BF16)
        else:
            parts = []
            for grp in range(c // 8):
                base = grp * 8
                if h == 4:
                    ref = row(base + (4 if rev else 3))
                elif h == 2:
                    ref = jnp.where(sub < 4, row(base + (2 if rev else 1)), row(base + (6 if rev else 5)))
                else:
                    k0 = base + (1 if rev else 0)
                    ref = jnp.where(sub < 2, row(k0), jnp.where(sub < 4, row(k0 + 2),
                                                               jnp.where(sub < 6, row(k0 + 4), row(k0 + 6))))
                parts.append(cum[base:base + 8] - ref)
            d = jnp.concatenate(parts, axis=0) * sgn_ref[HGRN_SIGN_LEVELS.index(h)]
        opnd = jnp.where(role_ref[HGRN_SELECT_LEVELS.index(h)] > 0, q, kk)
        return opnd * jnp.exp2(d).astype(BF16)

    def sides(item):
        if item == 0:
            return q, kk
        z = operand(item)
        return z, z

    zeros = jnp.zeros((c, HEAD_DIM), BF16)
    pairs = []
    for a, b in HGRN_PAIRS:
        la, ra = sides(a)
        if b is None:
            pairs.append((la, jnp.concatenate([ra, zeros], axis=0)))
        else:
            lb, rb = sides(b)
            pairs.append((jnp.concatenate([la, lb], axis=1),
                          jnp.concatenate([jnp.concatenate([ra, zeros], axis=1),
                                           jnp.concatenate([zeros, rb], axis=1)], axis=0)))
    return pairs


def _hgrn_blocks(x):
    return x[:HGRN_CHUNK, :HEAD_DIM], x[HGRN_CHUNK:, HEAD_DIM:]


def _hgrn_intra_products(pairs_f, pairs_b):
    out = [_hgrn_blocks(_dot_nt(jnp.concatenate([pf[0], pb[0]], axis=0), jnp.concatenate([pf[1], pb[1]], axis=0)))
           for pf, pb in zip(pairs_f, pairs_b)]
    return [o[0] for o in out], [o[1] for o in out]


def _hgrn_intra_attn(products, mw_ref, d):
    acc = None
    for j, t in enumerate(products):
        t = t.astype(BF16) * mw_ref[d, j]
        acc = t if acc is None else acc + t
    return acc


def _hgrn_outputs(fwd, bwd):
    (qf, kf, tf), af, vf, sf = fwd
    (qb, kb, tb), ab, vb, sb = bwd
    st_f, st_b = sf[...], sb[...]
    o_f, o_b = _hgrn_blocks(_dot_nt(jnp.concatenate([qf, qb], axis=0),
                                    jnp.concatenate([st_f.astype(BF16), st_b.astype(BF16)], axis=0)))
    sf[...] = st_f * jnp.exp2(tf) + _dot_tn(vf, kf)
    sb[...] = st_b * jnp.exp2(tb) + _dot_tn(vb, kb)
    i_f, i_b = _hgrn_blocks(_dot(jnp.concatenate([af, ab], axis=0),
                                 jnp.concatenate([jnp.concatenate([vf, vf], axis=0),
                                                  jnp.concatenate([vb, vb], axis=0)], axis=1)))
    return o_f + i_f, o_b + i_b


def _hgrn_kernel(*refs, with_ctx, m_ctx, n_lat):
    (qc_ref, kfc_ref, kbc_ref, ic_ref, gfc_ref, gbc_ref,
     ql_ref, kfl_ref, kbl_ref, il_ref, gfl_ref, gbl_ref) = refs[:12]
    refs = refs[12:]
    if with_ctx:
        zc_ref, refs = refs[0], refs[1:]
    zl_ref, gain_ref, tri_ref, mw_ref, role_ref, sgn_ref = refs[:6]
    refs = refs[6:]
    if with_ctx:
        oc_ref, refs = refs[0], refs[1:]
    ol_ref = refs[0]
    st_f, st_b, q_s, v_s, kkf_s, kkb_s, cumf_s, cumb_s, of_s, ob_s = refs[1:]
    chunk = HGRN_CHUNK
    nc_c, nc_l = m_ctx // chunk, n_lat // chunk

    def gather(q_ref, kf_ref, kb_ref, i_ref, gf_ref, gb_ref, n_chunks, base):
        def body(c, carry):
            r = pl.multiple_of(c * chunk, chunk)
            src = pl.ds(r, chunk)
            dst = pl.ds(pl.multiple_of(base + r, chunk), chunk)
            q_s[dst, :] = q_ref[src, :]
            v_s[dst, :] = i_ref[src, :]
            kkf_s[dst, :] = kf_ref[src, :]
            kkb_s[dst, :] = kb_ref[src, :]
            cumf_s[dst, :] = _hgrn_cumulate(gf_ref[src, :], tri_ref[0])
            cumb_s[dst, :] = _hgrn_cumulate(gb_ref[src, :], tri_ref[1])
            return carry
        lax.fori_loop(0, n_chunks, body, 0, unroll=8 if n_chunks % 8 == 0 else 4)

    gather(qc_ref, kfc_ref, kbc_ref, ic_ref, gfc_ref, gbc_ref, nc_c, 0)
    gather(ql_ref, kfl_ref, kbl_ref, il_ref, gfl_ref, gbl_ref, nc_l, m_ctx)

    st_f[...] = jnp.zeros_like(st_f)
    st_b[...] = jnp.zeros_like(st_b)
    n_chunks = nc_c + nc_l

    def rows(c):
        cb = jnp.where(c < nc_c, nc_c - 1 - c, n_chunks - 1 - (c - nc_c))
        return pl.multiple_of(c * chunk, chunk), pl.multiple_of(cb * chunk, chunk)

    def products(c):
        rf, rb = rows(c)
        return _hgrn_intra_products(
            _hgrn_pair_operands(q_s[pl.ds(rf, chunk), :], kkf_s[pl.ds(rf, chunk), :], cumf_s, rf,
                                role_ref.at[0], sgn_ref.at[0], False),
            _hgrn_pair_operands(q_s[pl.ds(rb, chunk), :], kkb_s[pl.ds(rb, chunk), :], cumb_s, rb,
                                role_ref.at[1], sgn_ref.at[1], True))

    def attn(prods):
        return _hgrn_intra_attn(prods[0], mw_ref, 0), _hgrn_intra_attn(prods[1], mw_ref, 1)

    def body(c, carry):
        attn_f, attn_b = carry
        nxt = products(jnp.minimum(c + 1, n_chunks - 1))
        rf, rb = rows(c)
        of_s[pl.ds(rf, chunk), :], ob_s[pl.ds(rb, chunk), :] = _hgrn_outputs(
            (_hgrn_state_operands(q_s[pl.ds(rf, chunk), :], kkf_s[pl.ds(rf, chunk), :], cumf_s, rf, False),
             attn_f, v_s[pl.ds(rf, chunk), :], st_f),
            (_hgrn_state_operands(q_s[pl.ds(rb, chunk), :], kkb_s[pl.ds(rb, chunk), :], cumb_s, rb, True),
             attn_b, v_s[pl.ds(rb, chunk), :], st_b))
        return attn(nxt)
    lax.fori_loop(0, n_chunks, body, attn(products(0)),
                  unroll=HGRN_UNROLL if n_chunks % HGRN_UNROLL == 0 else 1)

    def finish(z_ref, o_ref, rows, base):
        blk = HGRN_OUT_ROWS

        def body(i, carry):
            r = pl.multiple_of(i * blk, blk)
            src = pl.ds(pl.multiple_of(base + r, blk), blk)
            o = _rms((of_s[src, :] + ob_s[src, :]) * (HEAD_DIM ** -0.5), gain_ref[...])
            o_ref[pl.ds(r, blk), :] = (o * _silu(z_ref[pl.ds(r, blk), :].astype(F32))).astype(BF16)
            return carry
        lax.fori_loop(0, rows // blk, body, 0, unroll=2 if (rows // blk) % 2 == 0 else 1)

    if with_ctx:
        finish(zc_ref, oc_ref, m_ctx, 0)
    finish(zl_ref, ol_ref, n_lat, m_ctx)


def _hgrn_call(p_c, lf_c, p_l, lf_l, gain, batch, with_ctx):
    m = p_c.shape[0] // batch
    n = p_l.shape[0] // batch
    hd = HEAD_DIM
    assert m % 256 == 0 and n % 256 == 0
    consts = _hgrn_constants()

    def col(rows, c):
        return pl.BlockSpec((rows, hd), lambda b, h: (b, c // hd + h))

    def full(a):
        return pl.BlockSpec(a.shape, lambda b, h: (0,) * a.ndim)

    def side(rows, p, lf):
        return ([col(rows, C_Q), col(rows, C_ZF), col(rows, C_ZB), col(rows, C_I), col(rows, 0), col(rows, C_HEADS * hd)],
                [p, p, p, p, lf, lf])

    specs_c, args_c = side(m, p_c, lf_c)
    specs_l, args_l = side(n, p_l, lf_l)
    in_specs, args = specs_c + specs_l, args_c + args_l
    if with_ctx:
        in_specs.append(col(m, C_Z))
        args.append(p_c)
    in_specs += [col(n, C_Z), full(gain)] + [full(a) for a in consts]
    args += [p_l, gain, *consts]

    out_l_spec = pl.BlockSpec((n, hd), lambda b, h: (b, h))
    out_l_shape = jax.ShapeDtypeStruct((batch * n, Q_WIDTH), BF16)
    if with_ctx:
        out_specs = [pl.BlockSpec((m, hd), lambda b, h: (b, h)), out_l_spec]
        out_shape = [jax.ShapeDtypeStruct((batch * m, Q_WIDTH), BF16), out_l_shape]
    else:
        out_specs, out_shape = [out_l_spec], [out_l_shape]

    outs = pl.pallas_call(
        functools.partial(_hgrn_kernel, with_ctx=with_ctx, m_ctx=m, n_lat=n),
        grid=(batch, C_HEADS),
        in_specs=in_specs,
        out_specs=out_specs,
        out_shape=out_shape,
        scratch_shapes=([pltpu.VMEM((hd, hd), F32)] * 2 + [pltpu.VMEM((m + n, hd), BF16)] * 4
                        + [pltpu.VMEM((m + n, hd), F32)] * 4),
        compiler_params=_params("parallel", "parallel"),
        name="hgrn2",
    )(*args)
    return (outs[0], outs[1]) if with_ctx else (None, outs[0])


def _merge_kernel(*refs, tail):
    (x_ref, ga_ref, gb_ref, gc_ref, ka_ref, kb_ref, kc_ref, gt_ref, wa_ref, wb_ref, wc_ref, wo_ref) = refs[:12]
    mix = jax.nn.sigmoid(ka_ref[...].astype(F32)) * _dot(ga_ref[...], wa_ref[...])
    mix = mix + jax.nn.sigmoid(kb_ref[...].astype(F32)) * _dot(gb_ref[...], wb_ref[...])
    mix = mix + jax.nn.sigmoid(kc_ref[...].astype(F32)) * _dot(gc_ref[...], wc_ref[...])
    new = x_ref[...] + gt_ref[...] * _dot(mix.astype(BF16), wo_ref[...])
    if tail == "final":
        refs[13][...] = _rms(new, refs[12][...])
    else:
        g_ref, sc_ref, sh_ref, o_ref, h_ref = refs[12:]
        o_ref[...] = new
        h_ref[...] = _modulated_norm(new, g_ref[...], sc_ref[...], sh_ref[...]).astype(BF16)


def _merge_call(x, ga, gb, gc, p, mod, wa, wb, wc, wo, rows_per_mod, mod_row0, final_g=None, next_norm=None):
    rows, d = x.shape
    tm = _row_tile(rows, rows_per_mod, MERGE_TM)

    def rowblk(width, colblk=0):
        return pl.BlockSpec((tm, width), lambda i: (i, colblk))

    def weight(w):
        return pl.BlockSpec(w.shape, lambda i: (0, 0), pipeline_mode=pl.Buffered(1))

    def mod_spec(which):
        return pl.BlockSpec((None, None, 1, d), lambda i: (mod_row0 + (i * tm) // rows_per_mod, which, 0, 0))

    vec = pl.BlockSpec((1, d), lambda i: (0, 0))
    in_specs = [rowblk(d), rowblk(Q_WIDTH), rowblk(Q_WIDTH), rowblk(Q_WIDTH),
                rowblk(d, GATES // d), rowblk(d, GATES // d + 1), rowblk(d, GATES // d + 2),
                mod_spec(2), weight(wa), weight(wb), weight(wc), weight(wo)]
    args = [x, ga, gb, gc, p, p, p, mod, wa, wb, wc, wo]
    if final_g is not None:
        in_specs.append(vec)
        args.append(final_g.reshape(1, d))
        out_specs, out_shape = rowblk(d), jax.ShapeDtypeStruct((rows, d), F32)
    else:
        next_g, next_mod = next_norm
        in_specs += [vec, mod_spec(1), mod_spec(0)]
        args += [next_g.reshape(1, d), next_mod, next_mod]
        out_specs = [rowblk(d), rowblk(d)]
        out_shape = [jax.ShapeDtypeStruct((rows, d), F32), jax.ShapeDtypeStruct((rows, d), BF16)]
    return pl.pallas_call(
        functools.partial(_merge_kernel, tail="final" if final_g is not None else "next"),
        grid=(rows // tm,),
        in_specs=in_specs,
        out_specs=out_specs,
        out_shape=out_shape,
        compiler_params=_params("parallel"),
        name="merge_out",
    )(*args)


def _rope_tables(n):
    rows = n // GRID_W
    row = jnp.repeat(jnp.arange(rows, dtype=F32), GRID_W)
    col = jnp.tile(jnp.arange(GRID_W, dtype=F32), rows)
    n_freq = HEAD_DIM // 4
    inv_freq = ROPE_THETA ** (-jnp.arange(n_freq, dtype=F32) / n_freq)
    ang = jnp.concatenate([row[:, None] * inv_freq, col[:, None] * inv_freq], axis=-1)
    ang = jnp.concatenate([ang, ang], axis=-1)
    sign = jnp.where(jnp.arange(HEAD_DIM) < HEAD_DIM // 2, -1.0, 1.0).astype(F32)
    return jnp.cos(ang), jnp.sin(ang) * sign


def kernel(x, c, ctx, c_ctx, ada_w, ada_b, norm_g, w_in, a_sink, b_q_norm, b_k_norm, c_lower_bound, c_out_norm,
           w_branch_a, w_branch_b, w_branch_c, w_out, final_norm_g):
    batch, n, d = x.shape
    m = ctx.shape[1]
    depth = ada_w.shape[0]
    assert batch + 1 <= MOD_ROWS
    cos, sin = _rope_tables(n)
    lb_all = jnp.cumsum(jax.nn.softmax(c_lower_bound.astype(F32), axis=0), axis=0)
    lb_all = lb_all - lb_all[0:1]

    cv = jnp.zeros((MOD_ROWS, d), F32).at[:batch].set(c).at[batch].set(c_ctx)
    mod = _ada_call(cv, ada_w, ada_b).reshape(depth, MOD_ROWS, 3, 1, d)

    x_l = x.reshape(batch * n, d)
    x_c = ctx.reshape(batch * m, d)
    h_c = h_l = None
    for l in range(depth):
        with_ctx = l < depth - 1
        norm = (norm_g[l], mod[l]) if l == 0 else None
        p_c, lf_c = _inproj_call(x_c if l == 0 else h_c, norm, w_in, l, lb_all[l], batch * m, batch,
                                 col_tiles=None if with_ctx else LAST_LAYER_CTX_TILES)
        p_l, lf_l = _inproj_call(x_l if l == 0 else h_l, norm, w_in, l, lb_all[l], n, 0)
        ga_c, ga_l = _attn_call("a", p_c, p_l, (a_sink[l],), cos, sin, batch, with_ctx)
        gb_c, gb_l = _attn_call("b", p_c, p_l, (b_q_norm[l].reshape(1, -1), b_k_norm[l].reshape(1, -1)),
                                cos, sin, batch, with_ctx)
        gc_c, gc_l = _hgrn_call(p_c, lf_c, p_l, lf_l, c_out_norm[l].reshape(1, -1), batch, with_ctx)
        ws = (w_branch_a[l].astype(BF16), w_branch_b[l].astype(BF16), w_branch_c[l].astype(BF16),
              w_out[l].astype(BF16))
        if with_ctx:
            nxt = (norm_g[l + 1], mod[l + 1])
            x_c, h_c = _merge_call(x_c, ga_c, gb_c, gc_c, p_c, mod[l], *ws, batch * m, batch, next_norm=nxt)
            x_l, h_l = _merge_call(x_l, ga_l, gb_l, gc_l, p_l, mod[l], *ws, n, 0, next_norm=nxt)
        else:
            x_l = _merge_call(x_l, ga_l, gb_l, gc_l, p_l, mod[l], *ws, n, 0, final_g=final_norm_g)
    return x_l.reshape(batch, n, d)
```
